```python
import math
import jax, jax.numpy as jnp
from jax import lax
import numpy as np

D_MODEL = 2048
BATCH = 2
SEQ = 8192
DEPTH = 2
DEC_BATCH = 16
DEC_SEQ = 2048
PAST_LEN = 128

D_RNN = D_MODEL // 2
N_BLOCKS = 16
BLOCK_W = D_RNN // N_BLOCKS
CONV_RG = 4
RG_C = 8.0
H_GLA = 4
DK_TOT = D_MODEL // 4
DV_TOT = D_MODEL // 2
DK_HEAD = DK_TOT // H_GLA
DV_HEAD = DV_TOT // H_GLA
GATE_RANK = 16
GATE_NORM = 16.0
CHUNK = 64
D_FF = 3 * D_MODEL
CONV_FF = 3
D_IN = 2 * D_RNN + 2 * DK_TOT + 2 * DV_TOT + 2 * GATE_RANK + 2 * D_MODEL
ALPHA = (2.0 * DEPTH) ** 0.25
BETA = (8.0 * DEPTH) ** -0.25
EPS = 1e-5

kernel_name = "hybrid_rglru_gla_convffn_encoder"


def _layer_norm(x, g, b):
    xf = x.astype(jnp.float32)
    mu = jnp.mean(xf, axis=-1, keepdims=True)
    var = jnp.mean(jnp.square(xf - mu), axis=-1, keepdims=True)
    return ((xf - mu) * lax.rsqrt(var + EPS) * g + b).astype(x.dtype)


def _dwconv(x, w, b, pad_lo, pad_hi):
    s = x.shape[1]
    xp = jnp.pad(x, ((0, 0), (pad_lo, pad_hi), (0, 0)))
    out = b
    for j in range(w.shape[0]):
        out = out + w[j] * xp[:, j:j + s]
    return out


def _lin_combine(c1, c2):
    a1, b1 = c1
    a2, b2 = c2
    return a1 * a2, a2 * b1 + b2


def _rglru_dir(xc, wa, ba, wx, bx, lam, reverse):
    bsz, s, w = xc.shape
    xb = xc.reshape(bsz, s, N_BLOCKS, BLOCK_W)
    r = jax.nn.sigmoid(jnp.einsum('bsni,nij->bsnj', xb, wa).reshape(bsz, s, w) + ba)
    i = jax.nn.sigmoid(jnp.einsum('bsni,nij->bsnj', xb, wx).reshape(bsz, s, w) + bx)
    log_a = -RG_C * r * jax.nn.softplus(-lam)
    a = jnp.exp(log_a)
    u = jnp.sqrt(-jnp.expm1(2.0 * log_a)) * (i * xc)
    _, h = lax.associative_scan(_lin_combine, (a, u), axis=1, reverse=reverse)
    return h


def _gla_dir(q, k, v, g):
    bsz, h, s, dk = q.shape
    dv = v.shape[-1]
    n = s // CHUNK
    q = q.reshape(bsz, h, n, CHUNK, dk)
    k = k.reshape(bsz, h, n, CHUNK, dk)
    v = v.reshape(bsz, h, n, CHUNK, dv)
    gc = jnp.cumsum(g.reshape(bsz, h, n, CHUNK, dk), axis=3)
    g_last = gc[:, :, :, -1]
    q_d = q * jnp.exp(gc)
    k_d = k * jnp.exp(-gc)
    mask = jnp.tril(jnp.ones((CHUNK, CHUNK), dtype=bool))
    att = jnp.einsum('bhncd,bhnsd->bhncs', q_d, k_d)
    att = jnp.where(mask, att, 0.0)
    o_intra = jnp.einsum('bhncs,bhnsv->bhncv', att, v)
    k_end = k * jnp.exp(g_last[:, :, :, None, :] - gc)
    kv = jnp.einsum('bhncd,bhncv->bhndv', k_end, v)
    dec = jnp.exp(g_last)

    def step(state, inp):
        d_n, kv_n = inp
        return d_n[..., None] * state + kv_n, state

    init = jnp.zeros((bsz, h, dk, dv), dtype=q.dtype)
    _, s_prev = lax.scan(step, init, (jnp.moveaxis(dec, 2, 0), jnp.moveaxis(kv, 2, 0)))
    s_prev = jnp.moveaxis(s_prev, 0, 2)
    o_inter = jnp.einsum('bhncd,bhndv->bhncv', q_d, s_prev)
    return (o_intra + o_inter).reshape(bsz, h, s, dv)


def _token_mixer(x, w_in, b_in, conv_rg_w, conv_rg_b, rg_wa, rg_ba, rg_wx, rg_bx, rg_lam,
                 gla_wg2, gla_bg, gla_norm_w, w_proj_a, w_proj_b, w_out):
    bsz, s, _ = x.shape
    z = x @ w_in + b_in
    sizes = [D_RNN, D_RNN, DK_TOT, DK_TOT, DV_TOT, DV_TOT, GATE_RANK, GATE_RANK, D_MODEL, D_MODEL]
    offs = [int(o) for o in np.cumsum(sizes)[:-1]]
    z_rx, z_rg, z_q, z_k, z_v, z_og, z_gf, z_gb, z_ma, z_mb = jnp.split(z, offs, axis=-1)

    xa = _dwconv(z_rx, conv_rg_w, conv_rg_b, CONV_RG // 2, CONV_RG - 1 - CONV_RG // 2).astype(jnp.float32)
    h_a = (_rglru_dir(xa, rg_wa[0], rg_ba[0], rg_wx[0], rg_bx[0], rg_lam[0], False)
           + _rglru_dir(xa, rg_wa[1], rg_ba[1], rg_wx[1], rg_bx[1], rg_lam[1], True))
    y_a = (jax.nn.gelu(z_rg) * h_a.astype(x.dtype)) @ w_proj_a

    def heads(t, dh):
        return t.reshape(bsz, s, H_GLA, dh).transpose(0, 2, 1, 3).astype(jnp.float32)

    q = heads(z_q, DK_HEAD) * (DK_HEAD ** -0.5)
    k = heads(z_k, DK_HEAD)
    v = heads(z_v, DV_HEAD)
    g_f = heads(jax.nn.log_sigmoid(z_gf @ gla_wg2[0] + gla_bg[0]) / GATE_NORM, DK_HEAD)
    g_b = heads(jax.nn.log_sigmoid(z_gb @ gla_wg2[1] + gla_bg[1]) / GATE_NORM, DK_HEAD)
    o = _gla_dir(q, k, v, g_f) + jnp.flip(
        _gla_dir(jnp.flip(q, 2), jnp.flip(k, 2), jnp.flip(v, 2), jnp.flip(g_b, 2)), 2)
    o = o * lax.rsqrt(jnp.mean(jnp.square(o), axis=-1, keepdims=True) + EPS) * gla_norm_w
    o = o.transpose(0, 2, 1, 3).reshape(bsz, s, DV_TOT).astype(x.dtype)
    y_b = (o * jax.nn.silu(z_og)) @ w_proj_b

    merged = jax.nn.sigmoid(z_ma) * y_a + jax.nn.sigmoid(z_mb) * y_b
    return merged @ w_out


def _conv_ffn(x, w_up, conv_ff_w, conv_ff_b, w_down):
    u = x @ w_up
    u_g, u_v = jnp.split(u, 2, axis=-1)
    hdn = jax.nn.gelu(_dwconv(u_g, conv_ff_w, conv_ff_b, CONV_FF // 2, CONV_FF // 2)) * u_v
    return hdn @ w_down


def _trunk(x, ln_in_g, ln_in_b, w_in, b_in, conv_rg_w, conv_rg_b, rg_wa, rg_ba, rg_wx, rg_bx,
           rg_lam, gla_wg2, gla_bg, gla_norm_w, w_proj_a, w_proj_b, w_out, ln_mix_g, ln_mix_b,
           w_up, conv_ff_w, conv_ff_b, w_down, ln_ffn_g, ln_ffn_b):
    x = _layer_norm(x, ln_in_g, ln_in_b)
    for l in range(DEPTH):
        mix = _token_mixer(x, w_in[l], b_in[l], conv_rg_w[l], conv_rg_b[l], rg_wa[l], rg_ba[l],
                           rg_wx[l], rg_bx[l], rg_lam[l], gla_wg2[l], gla_bg[l], gla_norm_w[l],
                           w_proj_a[l], w_proj_b[l], w_out[l])
        x = _layer_norm(ALPHA * x + mix, ln_mix_g[l], ln_mix_b[l])
        ff = _conv_ffn(x, w_up[l], conv_ff_w[l], conv_ff_b[l], w_down[l])
        x = _layer_norm(ALPHA * x + ff, ln_ffn_g[l], ln_ffn_b[l])
    return x


def setup_inputs(seed: int = 0) -> dict:
    key = jax.random.key(seed)
    ks = jax.random.split(key, 32)
    f32 = jnp.float32

    def nrm(k, shape, scale):
        return jax.random.normal(k, shape, dtype=f32) * scale

    a0 = jax.random.uniform(ks[10], (DEPTH, 2, D_RNN), dtype=f32, minval=0.9, maxval=0.999)
    return {
        "x_prompt": nrm(ks[0], (BATCH, SEQ, D_MODEL), 1.0),
        "x_sample": nrm(ks[1], (DEC_BATCH, DEC_SEQ, D_MODEL), 1.0),
        "ln_in_g": 1.0 + nrm(ks[2], (D_MODEL,), 0.02),
        "ln_in_b": nrm(ks[3], (D_MODEL,), 0.02),
        "w_in": nrm(ks[4], (DEPTH, D_MODEL, D_IN), D_MODEL ** -0.5),
        "b_in": nrm(ks[5], (DEPTH, D_IN), 0.02),
        "conv_rg_w": nrm(ks[6], (DEPTH, CONV_RG, D_RNN), CONV_RG ** -0.5),
        "conv_rg_b": nrm(ks[7], (DEPTH, D_RNN), 0.02),
        "rg_wa": nrm(ks[8], (DEPTH, 2, N_BLOCKS, BLOCK_W, BLOCK_W), BLOCK_W ** -0.5),
        "rg_ba": nrm(ks[9], (DEPTH, 2, D_RNN), 0.02),
        "rg_wx": nrm(ks[11], (DEPTH, 2, N_BLOCKS, BLOCK_W, BLOCK_W), BLOCK_W ** -0.5),
        "rg_bx": nrm(ks[12], (DEPTH, 2, D_RNN), 0.02),
        "rg_lam": jnp.log(a0) - jnp.log1p(-a0),
        "gla_wg2": nrm(ks[13], (DEPTH, 2, GATE_RANK, DK_TOT), GATE_RANK ** -0.5),
        "gla_bg": nrm(ks[14], (DEPTH, 2, DK_TOT), 0.1),
        "gla_norm_w": 1.0 + nrm(ks[15], (DEPTH, DV_HEAD), 0.02),
        "w_proj_a": nrm(ks[16], (DEPTH, D_RNN, D_MODEL), BETA * D_RNN ** -0.5),
        "w_proj_b": nrm(ks[17], (DEPTH, DV_TOT, D_MODEL), BETA * DV_TOT ** -0.5),
        "w_out": nrm(ks[18], (DEPTH, D_MODEL, D_MODEL), BETA * D_MODEL ** -0.5),
        "ln_mix_g": 1.0 + nrm(ks[19], (DEPTH, D_MODEL), 0.02),
        "ln_mix_b": nrm(ks[20], (DEPTH, D_MODEL), 0.02),
        "w_up": nrm(ks[21], (DEPTH, D_MODEL, 2 * D_FF), D_MODEL ** -0.5),
        "conv_ff_w": nrm(ks[22], (DEPTH, CONV_FF, D_FF), CONV_FF ** -0.5),
        "conv_ff_b": nrm(ks[23], (DEPTH, D_FF), 0.02),
        "w_down": nrm(ks[24], (DEPTH, D_FF, D_MODEL), BETA * D_FF ** -0.5),
        "ln_ffn_g": 1.0 + nrm(ks[25], (DEPTH, D_MODEL), 0.02),
        "ln_ffn_b": nrm(ks[26], (DEPTH, D_MODEL), 0.02),
    }


def reference(x_prompt, x_sample, ln_in_g, ln_in_b, w_in, b_in, conv_rg_w, conv_rg_b, rg_wa,
              rg_ba, rg_wx, rg_bx, rg_lam, gla_wg2, gla_bg, gla_norm_w, w_proj_a, w_proj_b,
              w_out, ln_mix_g, ln_mix_b, w_up, conv_ff_w, conv_ff_b, w_down, ln_ffn_g, ln_ffn_b):
    y_prompt = _trunk(x_prompt, ln_in_g, ln_in_b, w_in, b_in, conv_rg_w, conv_rg_b, rg_wa, rg_ba,
                      rg_wx, rg_bx, rg_lam, gla_wg2, gla_bg, gla_norm_w, w_proj_a, w_proj_b, w_out,
                      ln_mix_g, ln_mix_b, w_up, conv_ff_w, conv_ff_b, w_down, ln_ffn_g, ln_ffn_b)
    y_sample = _trunk(x_sample, ln_in_g, ln_in_b, w_in, b_in, conv_rg_w, conv_rg_b, rg_wa, rg_ba,
                      rg_wx, rg_bx, rg_lam, gla_wg2, gla_bg, gla_norm_w, w_proj_a, w_proj_b, w_out,
                      ln_mix_g, ln_mix_b, w_up, conv_ff_w, conv_ff_b, w_down, ln_ffn_g, ln_ffn_b)
    return (y_prompt, y_sample)
```

```python
import functools
import math

import jax
import jax.numpy as jnp
from jax import lax
from jax.experimental import pallas as pl
from jax.experimental.pallas import tpu as pltpu

F32 = jnp.float32
BF16 = jnp.bfloat16

V7X_VMEM_BYTES = 64 * 1024 * 1024
LANES = 128
SUBLANES = 8
BF16_ROWS = 16

N_HEADS = 4
CHUNK = 64
RG_C = 8.0
GATE_NORM = 16.0
EPS = 1e-5
RG_GROUP = 128


def _vmem_limit(nbytes):
    return int(min(max(nbytes, 16 * 1024 * 1024), V7X_VMEM_BYTES - 6 * 1024 * 1024))


def _dot(a, b):
    return jnp.dot(a, b, preferred_element_type=F32)


def _layer_norm(x, g, b):
    mu = jnp.mean(x, axis=-1, keepdims=True)
    xc = x - mu
    var = jnp.mean(xc * xc, axis=-1, keepdims=True)
    return xc * lax.rsqrt(var + EPS) * g + b


def _softplus(x):
    return jnp.maximum(x, 0.0) + jnp.log1p(jnp.exp(-jnp.abs(x)))


def _gelu_tanh(x):
    c = math.sqrt(2.0 / math.pi)
    return 0.5 * x * (1.0 + jnp.tanh(c * (x + 0.044715 * (x * x * x))))


def _pick_tile(n, target, multiple):
    t = min(n, target)
    while t > multiple and (n % t or t % multiple):
        t -= multiple
    assert n % t == 0 and t % multiple == 0, (n, target, multiple)
    return t


def _ln_kernel(y_ref, g_ref, b_ref, o_ref):
    o_ref[...] = _layer_norm(y_ref[...], g_ref[...], b_ref[...])


def _ln_call(y, g, b):
    t, d = y.shape
    tm = _pick_tile(t, 512, SUBLANES)
    return pl.pallas_call(
        _ln_kernel,
        grid=(t // tm,),
        in_specs=[pl.BlockSpec((tm, d), lambda i: (i, 0)),
                  pl.BlockSpec((1, d), lambda i: (0, 0)),
                  pl.BlockSpec((1, d), lambda i: (0, 0))],
        out_specs=pl.BlockSpec((tm, d), lambda i: (i, 0)),
        out_shape=jax.ShapeDtypeStruct((t, d), F32),
        compiler_params=pltpu.CompilerParams(
            dimension_semantics=("parallel",),
            vmem_limit_bytes=_vmem_limit(6 * tm * d * 4)),
        name="final_ln",
    )(y, g, b)


def _ln_matmul_in_kernel(y_ref, g_ref, b_ref, w_ref, bias_ref, wg_ref, bg_ref,
                         z_ref, zg_ref, xb_ref):
    @pl.when(pl.program_id(1) == 0)
    def _():
        xb = _layer_norm(y_ref[...], g_ref[...], b_ref[...]).astype(BF16)
        xb_ref[...] = xb
        zg_ref[...] = _dot(xb, wg_ref[...]) + bg_ref[...]

    z_ref[...] = _dot(xb_ref[...], w_ref[...]) + bias_ref[...]


def _ln_matmul_in_call(y, g, b, w, bias, wg, bg):
    t, d = y.shape
    n = w.shape[1]
    tm = _pick_tile(t, 1024, BF16_ROWS)
    tn = _pick_tile(n, 1024, LANES)
    est = (2 * tm * d * 4 + tm * d * 2 + 2 * d * tn * 2 + 3 * tm * tn * 4
           + 2 * d * LANES * 2 + 2 * tm * LANES * 4 + 3 * tm * d * 4)
    return pl.pallas_call(
        _ln_matmul_in_kernel,
        grid=(t // tm, n // tn),
        in_specs=[pl.BlockSpec((tm, d), lambda i, j: (i, 0)),
                  pl.BlockSpec((1, d), lambda i, j: (0, 0)),
                  pl.BlockSpec((1, d), lambda i, j: (0, 0)),
                  pl.BlockSpec((d, tn), lambda i, j: (0, j)),
                  pl.BlockSpec((1, tn), lambda i, j: (0, j)),
                  pl.BlockSpec((d, LANES), lambda i, j: (0, 0)),
                  pl.BlockSpec((1, LANES), lambda i, j: (0, 0))],
        out_specs=[pl.BlockSpec((tm, tn), lambda i, j: (i, j)),
                   pl.BlockSpec((tm, LANES), lambda i, j: (i, 0))],
        out_shape=[jax.ShapeDtypeStruct((t, n), F32),
                   jax.ShapeDtypeStruct((t, LANES), F32)],
        scratch_shapes=[pltpu.VMEM((tm, d), BF16)],
        compiler_params=pltpu.CompilerParams(
            dimension_semantics=("parallel", "arbitrary"),
            vmem_limit_bytes=_vmem_limit(est)),
        name="ln_matmul_in",
    )(y, g, b, w, bias, wg, bg)


def _scan_tile(a, u, h0, reverse):
    rows, c = a.shape
    groups = rows // SUBLANES
    a3 = a.reshape(groups, SUBLANES, c)
    u3 = u.reshape(groups, SUBLANES, c)
    row = lax.broadcasted_iota(jnp.int32, (groups, SUBLANES, c), 1)
    for d in (1, 2, 4):
        shift = SUBLANES - d if reverse else d
        valid = (row < SUBLANES - d) if reverse else (row >= d)
        a_s = pltpu.roll(a3, shift, 1)
        u_s = pltpu.roll(u3, shift, 1)
        u3 = jnp.where(valid, a3 * u_s + u3, u3)
        a3 = jnp.where(valid, a3 * a_s, a3)
    outs = [None] * groups
    carry = h0
    for g in (range(groups - 1, -1, -1) if reverse else range(groups)):
        hg = a3[g] * carry + u3[g]
        outs[g] = hg
        carry = hg[0:1] if reverse else hg[SUBLANES - 1:SUBLANES]
    return jnp.concatenate(outs, axis=0), carry


def _rglru_kernel(zrx_ref, zrg_ref, cw_ref, cb_ref, wg_ref, bg_ref, lam_ref, o_ref,
                  xpad_ref, xa_ref, hf_ref, *, tile):
    s, c = zrx_ref.shape
    n_tiles = s // tile
    pad = SUBLANES

    xpad_ref[0:pad, :] = jnp.zeros((pad, c), F32)
    xpad_ref[pad + s:pad + s + pad, :] = jnp.zeros((pad, c), F32)

    def copy_body(i, _):
        t0 = pl.multiple_of(i * tile, tile)
        xpad_ref[pl.ds(pad + t0, tile), :] = zrx_ref[pl.ds(t0, tile), :]
        return 0

    lax.fori_loop(0, n_tiles, copy_body, 0)

    cw = cw_ref[...]
    cb = cb_ref[...]
    bg = bg_ref[...]
    decay = -RG_C * _softplus(-lam_ref[...])

    def gates(xa, direction):
        lo = 2 * c * direction
        gz = _dot(xa.astype(BF16), wg_ref[:, lo:lo + 2 * c]) + bg[:, lo:lo + 2 * c]
        r = jax.nn.sigmoid(gz[:, :c])
        ig = jax.nn.sigmoid(gz[:, c:])
        log_a = r * decay[:, c * direction:c * (direction + 1)]
        a = jnp.exp(log_a)
        th = jnp.tanh(log_a)
        one_minus_a2 = (-2.0 * th) / (1.0 - th)
        u = jnp.sqrt(one_minus_a2) * (ig * xa)
        return a, u

    def fwd_body(i, h0):
        t0 = pl.multiple_of(i * tile, tile)
        xa = cb
        for j in range(4):
            xa = xa + cw[j:j + 1, :] * xpad_ref[pl.ds(pad + t0 - 2 + j, tile), :]
        xa_ref[pl.ds(t0, tile), :] = xa
        a, u = gates(xa, 0)
        h, h_last = _scan_tile(a, u, h0, reverse=False)
        hf_ref[pl.ds(t0, tile), :] = h
        return h_last

    lax.fori_loop(0, n_tiles, fwd_body, jnp.zeros((1, c), F32))

    def bwd_body(i, h0):
        t0 = pl.multiple_of((n_tiles - 1 - i) * tile, tile)
        xa = xa_ref[pl.ds(t0, tile), :]
        a, u = gates(xa, 1)
        h, h_first = _scan_tile(a, u, h0, reverse=True)
        h = h + hf_ref[pl.ds(t0, tile), :]
        o_ref[pl.ds(t0, tile), :] = (_gelu_tanh(zrg_ref[pl.ds(t0, tile), :]) * h).astype(o_ref.dtype)
        return h_first

    lax.fori_loop(0, n_tiles, bwd_body, jnp.zeros((1, c), F32))


def _rglru_call(z3, cw, cb, wg, bg, lam, d_rnn):
    b, s, _ = z3.shape
    c = RG_GROUP
    n_groups = d_rnn // c
    tile = _pick_tile(s, 128, SUBLANES)
    est = (2 * 2 * s * c * 4 + 2 * s * c * 2 + 3 * s * c * 4 + 2 * SUBLANES * c * 4
           + 24 * tile * 4 * c * 4)
    return pl.pallas_call(
        functools.partial(_rglru_kernel, tile=tile),
        grid=(b, n_groups),
        in_specs=[pl.BlockSpec((None, s, c), lambda i, j: (i, 0, j)),
                  pl.BlockSpec((None, s, c), lambda i, j: (i, 0, n_groups + j)),
                  pl.BlockSpec((None, 4, c), lambda i, j: (j, 0, 0)),
                  pl.BlockSpec((None, 1, c), lambda i, j: (j, 0, 0)),
                  pl.BlockSpec((None, c, 4 * c), lambda i, j: (j, 0, 0)),
                  pl.BlockSpec((None, 1, 4 * c), lambda i, j: (j, 0, 0)),
                  pl.BlockSpec((None, 1, 2 * c), lambda i, j: (j, 0, 0))],
        out_specs=pl.BlockSpec((None, s, c), lambda i, j: (i, 0, j)),
        out_shape=jax.ShapeDtypeStruct((b, s, d_rnn), BF16),
        scratch_shapes=[pltpu.VMEM((s + 2 * SUBLANES, c), F32),
                        pltpu.VMEM((s, c), F32),
                        pltpu.VMEM((s, c), F32)],
        compiler_params=pltpu.CompilerParams(
            dimension_semantics=("parallel", "parallel"),
            vmem_limit_bytes=_vmem_limit(est)),
        name="rglru",
    )(z3, z3, cw, cb, wg, bg, lam)


def _chunk_cumsum(g, reverse):
    rows, c = g.shape
    groups = rows // SUBLANES
    per_chunk = CHUNK // SUBLANES
    x = g.reshape(groups, SUBLANES, c)
    row = lax.broadcasted_iota(jnp.int32, (groups, SUBLANES, c), 1)
    for d in (1, 2, 4):
        shift = SUBLANES - d if reverse else d
        valid = (row < SUBLANES - d) if reverse else (row >= d)
        x = x + jnp.where(valid, pltpu.roll(x, shift, 1), 0.0)
    outs = [None] * groups
    for ch in range(groups // per_chunk):
        carry = None
        order = range(per_chunk - 1, -1, -1) if reverse else range(per_chunk)
        for gi in order:
            idx = ch * per_chunk + gi
            xg = x[idx] if carry is None else x[idx] + carry
            outs[idx] = xg
            carry = xg[0:1] if reverse else xg[SUBLANES - 1:SUBLANES]
    return jnp.concatenate(outs, axis=0)


def _gla_kernel(*refs, reverse, finalize, scale):
    if finalize:
        (q_ref, k_ref, v_ref, zg_ref, wg_ref, bg_ref, of_ref, og_ref, nw_ref,
         o_ref, st_ref) = refs
    else:
        q_ref, k_ref, v_ref, zg_ref, wg_ref, bg_ref, o_ref, st_ref = refs
    rows, dk = q_ref.shape
    n_chunks = rows // CHUNK

    @pl.when(pl.program_id(2) == 0)
    def _():
        st_ref[...] = jnp.zeros_like(st_ref)

    gz = _dot(zg_ref[...].astype(BF16), wg_ref[...]) + bg_ref[...]
    g = -_softplus(-gz) / GATE_NORM
    gc = _chunk_cumsum(g, reverse)
    q = q_ref[...] * scale
    k = k_ref[...]
    qd = (q * jnp.exp(gc)).astype(BF16)
    kd = (k * jnp.exp(-gc)).astype(BF16)

    ti = lax.broadcasted_iota(jnp.int32, (CHUNK, CHUNK), 0)
    si = lax.broadcasted_iota(jnp.int32, (CHUNK, CHUNK), 1)
    mask = (si >= ti) if reverse else (si <= ti)

    st = st_ref[...]
    order = range(n_chunks - 1, -1, -1) if reverse else range(n_chunks)
    outs = [None] * n_chunks
    for ch in order:
        lo = ch * CHUNK
        edge = lo if reverse else lo + CHUNK - 1
        g_last = gc[edge:edge + 1, :]
        v = v_ref[lo:lo + CHUNK, :].astype(BF16)
        qd_c = qd[lo:lo + CHUNK]
        k_end = (k[lo:lo + CHUNK] * jnp.exp(g_last - gc[lo:lo + CHUNK])).astype(BF16)
        att = lax.dot_general(qd_c, kd[lo:lo + CHUNK], (((1,), (1,)), ((), ())),
                              preferred_element_type=F32)
        att = jnp.where(mask, att, 0.0).astype(BF16)
        o = _dot(att, v) + lax.dot_general(qd_c, st.astype(BF16), (((1,), (1,)), ((), ())),
                                           preferred_element_type=F32)
        outs[ch] = o
        kv_t = lax.dot_general(v, k_end, (((0,), (0,)), ((), ())),
                               preferred_element_type=F32)
        st = jnp.exp(g_last) * st + kv_t
    st_ref[...] = st
    o = jnp.concatenate(outs, axis=0)

    if finalize:
        o = o + of_ref[...]
        o = o * lax.rsqrt(jnp.mean(o * o, axis=-1, keepdims=True) + EPS) * nw_ref[...]
        og = og_ref[...]
        o_ref[...] = (o * (og * jax.nn.sigmoid(og))).astype(o_ref.dtype)
    else:
        o_ref[...] = o


def _gla_call(z3, zg3, wg, bg, *, reverse, col_q, col_k, col_v, col_og, dk, dv,
              o_fwd=None, norm_w=None):
    b, s, _ = z3.shape
    rows = _pick_tile(s, 512, CHUNK)
    nb = s // rows
    finalize = o_fwd is not None

    def blk(j):
        return nb - 1 - j if reverse else j

    in_specs = [pl.BlockSpec((None, rows, dk), lambda i, h, j: (i, blk(j), col_q // dk + h)),
                pl.BlockSpec((None, rows, dk), lambda i, h, j: (i, blk(j), col_k // dk + h)),
                pl.BlockSpec((None, rows, dv), lambda i, h, j: (i, blk(j), col_v // dv + h)),
                pl.BlockSpec((None, rows, LANES), lambda i, h, j: (i, blk(j), 0)),
                pl.BlockSpec((None, LANES, dk), lambda i, h, j: (h, 0, 0)),
                pl.BlockSpec((None, 1, dk), lambda i, h, j: (h, 0, 0))]
    args = [z3, z3, z3, zg3, wg, bg]
    if finalize:
        in_specs += [pl.BlockSpec((None, rows, dv), lambda i, h, j: (i, blk(j), h)),
                     pl.BlockSpec((None, rows, dv), lambda i, h, j: (i, blk(j), col_og // dv + h)),
                     pl.BlockSpec((1, dv), lambda i, h, j: (0, 0))]
        args += [o_fwd, z3, norm_w]
    est = 2 * rows * (2 * dk + 3 * dv + LANES) * 4 + 40 * rows * dk * 4 + 8 * dv * dk * 4
    return pl.pallas_call(
        functools.partial(_gla_kernel, reverse=reverse, finalize=finalize, scale=dk ** -0.5),
        grid=(b, N_HEADS, nb),
        in_specs=in_specs,
        out_specs=pl.BlockSpec((None, rows, dv), lambda i, h, j: (i, blk(j), h)),
        out_shape=jax.ShapeDtypeStruct((b, s, N_HEADS * dv), BF16 if finalize else F32),
        scratch_shapes=[pltpu.VMEM((dv, dk), F32)],
        compiler_params=pltpu.CompilerParams(
            dimension_semantics=("parallel", "parallel", "arbitrary"),
            vmem_limit_bytes=_vmem_limit(est)),
        name="gla_bwd" if reverse else "gla_fwd",
    )(*args)


def _mix_merge_kernel(ya_ref, yb_ref, ma_ref, mb_ref, y_ref, g_ref, b_ref,
                      wa_ref, wb_ref, wo_ref, o_ref, *, alpha):
    y_a = _dot(ya_ref[...], wa_ref[...])
    y_b = _dot(yb_ref[...], wb_ref[...])
    merged = jax.nn.sigmoid(ma_ref[...]) * y_a + jax.nn.sigmoid(mb_ref[...]) * y_b
    mix = _dot(merged.astype(BF16), wo_ref[...])
    x = _layer_norm(y_ref[...], g_ref[...], b_ref[...])
    o_ref[...] = alpha * x + mix


def _mix_merge_call(ya, yb, z, y, g, b, wa, wb, wo, *, col_ma, col_mb, alpha):
    t, d = y.shape
    dr = ya.shape[1]
    tm = _pick_tile(t, 256, BF16_ROWS)
    est = (2 * 2 * tm * dr * 2 + 2 * 4 * tm * d * 4 + (2 * dr * d + d * d) * 2 * 2
           + 6 * tm * d * 4)
    const = lambda i: (0, 0)
    return pl.pallas_call(
        functools.partial(_mix_merge_kernel, alpha=alpha),
        grid=(t // tm,),
        in_specs=[pl.BlockSpec((tm, dr), lambda i: (i, 0)),
                  pl.BlockSpec((tm, dr), lambda i: (i, 0)),
                  pl.BlockSpec((tm, d), lambda i: (i, col_ma // d)),
                  pl.BlockSpec((tm, d), lambda i: (i, col_mb // d)),
                  pl.BlockSpec((tm, d), lambda i: (i, 0)),
                  pl.BlockSpec((1, d), const),
                  pl.BlockSpec((1, d), const),
                  pl.BlockSpec((dr, d), const),
                  pl.BlockSpec((dr, d), const),
                  pl.BlockSpec((d, d), const)],
        out_specs=pl.BlockSpec((tm, d), lambda i: (i, 0)),
        out_shape=jax.ShapeDtypeStruct((t, d), F32),
        compiler_params=pltpu.CompilerParams(
            dimension_semantics=("parallel",),
            vmem_limit_bytes=_vmem_limit(est)),
        name="mix_merge",
    )(ya, yb, z, z, y, g, b, wa, wb, wo)


def _ffn_up_kernel(y_ref, yp_ref, yn_ref, g_ref, b_ref, wg_ref, wv_ref, cw_ref, cb_ref,
                   o_ref, xs_ref, *, tiles_per_seq):
    tm = y_ref.shape[0]
    halo = yp_ref.shape[0]
    i = pl.program_id(0)

    @pl.when(pl.program_id(1) == 0)
    def _():
        g = g_ref[...]
        b = b_ref[...]
        xs_ref[halo:halo + tm, :] = _layer_norm(y_ref[...], g, b).astype(BF16)
        has_prev = (i % tiles_per_seq) != 0
        has_next = (i % tiles_per_seq) != tiles_per_seq - 1
        xp = _layer_norm(yp_ref[...], g, b)
        xn = _layer_norm(yn_ref[...], g, b)
        xs_ref[0:halo, :] = jnp.where(has_prev, xp, 0.0).astype(BF16)
        xs_ref[halo + tm:halo + tm + halo, :] = jnp.where(has_next, xn, 0.0).astype(BF16)

    ug = _dot(xs_ref[...], wg_ref[...])
    uv = _dot(xs_ref[halo:halo + tm, :], wv_ref[...])
    n_rows = ug.shape[0]
    cw = cw_ref[...]
    conv = (cb_ref[...] + cw[0:1, :] * pltpu.roll(ug, 1, 0) + cw[1:2, :] * ug
            + cw[2:3, :] * pltpu.roll(ug, n_rows - 1, 0))
    conv = conv[halo:halo + tm, :]
    o_ref[...] = (_gelu_tanh(conv) * uv).astype(o_ref.dtype)


def _ffn_up_call(y, g, b, w_up, cw, cb, seq_len):
    t, d = y.shape
    d_ff = w_up.shape[1] // 2
    halo = BF16_ROWS
    tm = _pick_tile(seq_len, 1024, halo)
    tn = _pick_tile(d_ff, 512, LANES)
    tiles_per_seq = seq_len // tm
    hb = tm // halo
    n_hblocks = t // halo
    n_j = d_ff // tn
    est = (2 * tm * d * 4 + 4 * halo * d * 4 + (tm + 2 * halo) * d * 2 + 2 * 2 * d * tn * 2
           + 2 * tm * tn * 2 + 8 * (tm + 2 * halo) * tn * 4 + 3 * tm * d * 4)
    return pl.pallas_call(
        functools.partial(_ffn_up_kernel, tiles_per_seq=tiles_per_seq),
        grid=(t // tm, n_j),
        in_specs=[pl.BlockSpec((tm, d), lambda i, j: (i, 0)),
                  pl.BlockSpec((halo, d), lambda i, j: (jnp.maximum(i * hb - 1, 0), 0)),
                  pl.BlockSpec((halo, d), lambda i, j: (jnp.minimum((i + 1) * hb, n_hblocks - 1), 0)),
                  pl.BlockSpec((1, d), lambda i, j: (0, 0)),
                  pl.BlockSpec((1, d), lambda i, j: (0, 0)),
                  pl.BlockSpec((d, tn), lambda i, j: (0, j)),
                  pl.BlockSpec((d, tn), lambda i, j: (0, n_j + j)),
                  pl.BlockSpec((3, tn), lambda i, j: (0, j)),
                  pl.BlockSpec((1, tn), lambda i, j: (0, j))],
        out_specs=pl.BlockSpec((tm, tn), lambda i, j: (i, j)),
        out_shape=jax.ShapeDtypeStruct((t, d_ff), BF16),
        scratch_shapes=[pltpu.VMEM((tm + 2 * halo, d), BF16)],
        compiler_params=pltpu.CompilerParams(
            dimension_semantics=("parallel", "arbitrary"),
            vmem_limit_bytes=_vmem_limit(est)),
        name="ffn_up",
    )(y, y, y, g, b, w_up, w_up, cw, cb)


def _ffn_down_kernel(h_ref, w_ref, y_ref, g_ref, b_ref, o_ref, acc_ref, *, alpha):
    k = pl.program_id(1)

    @pl.when(k == 0)
    def _():
        acc_ref[...] = alpha * _layer_norm(y_ref[...], g_ref[...], b_ref[...])

    acc_ref[...] += _dot(h_ref[...], w_ref[...])

    @pl.when(k == pl.num_programs(1) - 1)
    def _():
        o_ref[...] = acc_ref[...]


def _ffn_down_call(h, w, y, g, b, *, alpha):
    t, d = y.shape
    d_ff = h.shape[1]
    tm = _pick_tile(t, 512, BF16_ROWS)
    tk = _pick_tile(d_ff, 2048, LANES)
    est = 2 * tm * tk * 2 + 2 * tk * d * 2 + 5 * tm * d * 4 + 4 * tm * d * 4
    return pl.pallas_call(
        functools.partial(_ffn_down_kernel, alpha=alpha),
        grid=(t // tm, d_ff // tk),
        in_specs=[pl.BlockSpec((tm, tk), lambda i, k: (i, k)),
                  pl.BlockSpec((tk, d), lambda i, k: (k, 0)),
                  pl.BlockSpec((tm, d), lambda i, k: (i, 0)),
                  pl.BlockSpec((1, d), lambda i, k: (0, 0)),
                  pl.BlockSpec((1, d), lambda i, k: (0, 0))],
        out_specs=pl.BlockSpec((tm, d), lambda i, k: (i, 0)),
        out_shape=jax.ShapeDtypeStruct((t, d), F32),
        scratch_shapes=[pltpu.VMEM((tm, d), F32)],
        compiler_params=pltpu.CompilerParams(
            dimension_semantics=("parallel", "arbitrary"),
            vmem_limit_bytes=_vmem_limit(est)),
        name="ffn_down",
    )(h, w, y, g, b)


def _prepare_layer(p, l, d_model):
    d_rnn = p["conv_rg_w"].shape[-1]
    dk_tot = p["gla_wg2"].shape[-1]
    rank = p["gla_wg2"].shape[-2]
    dv_tot = p["w_proj_b"].shape[1]
    n_blocks, block_w = p["rg_wa"].shape[2], p["rg_wa"].shape[3]
    dk, dv = dk_tot // N_HEADS, dv_tot // N_HEADS

    sizes = [d_rnn, d_rnn, dk_tot, dk_tot, dv_tot, dv_tot, rank, rank, d_model, d_model]
    offs = [0]
    for sz in sizes:
        offs.append(offs[-1] + sz)
    rx, rg, q, k, v, og, gf, gb, ma, mb = [slice(offs[i], offs[i + 1]) for i in range(10)]
    order = [rx, rg, ma, mb, q, k, v, og]
    w_in, b_in = p["w_in"][l], p["b_in"][l]
    w_main = jnp.concatenate([w_in[:, s] for s in order], axis=1).astype(BF16)
    b_main = jnp.concatenate([b_in[s] for s in order])[None, :]
    cols, c0 = {}, 0
    for name, s in zip(("rx", "rg", "ma", "mb", "q", "k", "v", "og"), order):
        cols[name] = c0
        c0 += s.stop - s.start
    w_gate = jnp.zeros((d_model, LANES), F32).at[:, :2 * rank].set(w_in[:, gf.start:gb.stop]).astype(BF16)
    b_gate = jnp.zeros((1, LANES), F32).at[0, :2 * rank].set(b_in[gf.start:gb.stop])

    c = RG_GROUP
    n_groups = d_rnn // c
    per = c // block_w

    def group_dense(w):
        wgrp = w.reshape(n_groups, per, block_w, block_w)
        eye = jnp.eye(per, dtype=w.dtype)
        return jnp.einsum("gpij,pq->gpiqj", wgrp, eye).reshape(n_groups, c, c)

    rg_w = jnp.concatenate([group_dense(p["rg_wa"][l, 0]), group_dense(p["rg_wx"][l, 0]),
                            group_dense(p["rg_wa"][l, 1]), group_dense(p["rg_wx"][l, 1])],
                           axis=2).astype(BF16)

    def grp(vec):
        return vec.reshape(n_groups, 1, c)

    rg_b = jnp.concatenate([grp(p["rg_ba"][l, 0]), grp(p["rg_bx"][l, 0]),
                            grp(p["rg_ba"][l, 1]), grp(p["rg_bx"][l, 1])], axis=2)
    rg_lam = jnp.concatenate([grp(p["rg_lam"][l, 0]), grp(p["rg_lam"][l, 1])], axis=2)
    conv_w = p["conv_rg_w"][l].reshape(-1, n_groups, c).transpose(1, 0, 2)
    conv_b = grp(p["conv_rg_b"][l])

    def gate_w(direction):
        wpad = jnp.zeros((LANES, dk_tot), F32)
        wpad = wpad.at[direction * rank:(direction + 1) * rank].set(p["gla_wg2"][l, direction])
        return wpad.reshape(LANES, N_HEADS, dk).transpose(1, 0, 2).astype(BF16)

    def gate_b(direction):
        return p["gla_bg"][l, direction].reshape(N_HEADS, 1, dk)

    return dict(
        cols=cols, d_rnn=d_rnn, dk=dk, dv=dv,
        w_main=w_main, b_main=b_main, w_gate=w_gate, b_gate=b_gate,
        rg_w=rg_w, rg_b=rg_b, rg_lam=rg_lam, conv_w=conv_w, conv_b=conv_b,
        gla_w=(gate_w(0), gate_w(1)), gla_b=(gate_b(0), gate_b(1)),
        norm_w=p["gla_norm_w"][l][None, :],
        w_pa=p["w_proj_a"][l].astype(BF16), w_pb=p["w_proj_b"][l].astype(BF16),
        w_out=p["w_out"][l].astype(BF16),
        ln_mix=(p["ln_mix_g"][l][None, :], p["ln_mix_b"][l][None, :]),
        w_up=p["w_up"][l].astype(BF16), conv_ff_w=p["conv_ff_w"][l],
        conv_ff_b=p["conv_ff_b"][l][None, :], w_down=p["w_down"][l].astype(BF16),
        ln_ffn=(p["ln_ffn_g"][l][None, :], p["ln_ffn_b"][l][None, :]),
    )


def _trunk(x, ln_in, layers, alpha):
    b, s, d = x.shape
    y = x.reshape(b * s, d)
    ln = ln_in
    for lp in layers:
        cols = lp["cols"]
        z, zg = _ln_matmul_in_call(y, ln[0], ln[1], lp["w_main"], lp["b_main"],
                                   lp["w_gate"], lp["b_gate"])
        z3 = z.reshape(b, s, z.shape[1])
        zg3 = zg.reshape(b, s, LANES)
        ya = _rglru_call(z3, lp["conv_w"], lp["conv_b"], lp["rg_w"], lp["rg_b"], lp["rg_lam"],
                         lp["d_rnn"])
        gla = functools.partial(_gla_call, z3, zg3, col_q=cols["q"], col_k=cols["k"],
                                col_v=cols["v"], col_og=cols["og"], dk=lp["dk"], dv=lp["dv"])
        o_f = gla(lp["gla_w"][0], lp["gla_b"][0], reverse=False)
        yb = gla(lp["gla_w"][1], lp["gla_b"][1], reverse=True, o_fwd=o_f, norm_w=lp["norm_w"])
        y_mix = _mix_merge_call(ya.reshape(b * s, -1), yb.reshape(b * s, -1), z, y, ln[0], ln[1],
                                lp["w_pa"], lp["w_pb"], lp["w_out"],
                                col_ma=cols["ma"], col_mb=cols["mb"], alpha=alpha)
        hdn = _ffn_up_call(y_mix, lp["ln_mix"][0], lp["ln_mix"][1], lp["w_up"],
                           lp["conv_ff_w"], lp["conv_ff_b"], s)
        y = _ffn_down_call(hdn, lp["w_down"], y_mix, lp["ln_mix"][0], lp["ln_mix"][1], alpha=alpha)
        ln = lp["ln_ffn"]
    return _ln_call(y, ln[0], ln[1]).reshape(b, s, d)


def kernel(x_prompt, x_sample, ln_in_g, ln_in_b, w_in, b_in, conv_rg_w, conv_rg_b, rg_wa, rg_ba, rg_wx, rg_bx, rg_lam, gla_wg2, gla_bg, gla_norm_w, w_proj_a, w_proj_b, w_out, ln_mix_g, ln_mix_b, w_up, conv_ff_w, conv_ff_b, w_down, ln_ffn_g, ln_ffn_b):
    p = dict(w_in=w_in, b_in=b_in, conv_rg_w=conv_rg_w, conv_rg_b=conv_rg_b, rg_wa=rg_wa,
             rg_ba=rg_ba, rg_wx=rg_wx, rg_bx=rg_bx, rg_lam=rg_lam, gla_wg2=gla_wg2, gla_bg=gla_bg,
             gla_norm_w=gla_norm_w, w_proj_a=w_proj_a, w_proj_b=w_proj_b, w_out=w_out,
             ln_mix_g=ln_mix_g, ln_mix_b=ln_mix_b, w_up=w_up, conv_ff_w=conv_ff_w,
             conv_ff_b=conv_ff_b, w_down=w_down, ln_ffn_g=ln_ffn_g, ln_ffn_b=ln_ffn_b)
    depth = w_in.shape[0]
    d_model = w_in.shape[1]
    alpha = (2.0 * depth) ** 0.25
    layers = [_prepare_layer(p, l, d_model) for l in range(depth)]
    ln_in = (ln_in_g[None, :], ln_in_b[None, :])
    return (_trunk(x_prompt, ln_in, layers, alpha), _trunk(x_sample, ln_in, layers, alpha))
```

```python
import functools
import math

import jax
import jax.numpy as jnp
from jax import lax
from jax.experimental import pallas as pl
from jax.experimental.pallas import tpu as pltpu

F32 = jnp.float32
BF16 = jnp.bfloat16

V7X_VMEM_BYTES = 64 * 1024 * 1024
LANES = 128
SUBLANES = 8
BF16_ROWS = 16

N_HEADS = 4
CHUNK = 64
RG_C = 8.0
GATE_NORM = 16.0
EPS = 1e-5
RG_GROUP = 128


def _vmem_limit(nbytes):
    return int(min(max(nbytes, 16 * 1024 * 1024), V7X_VMEM_BYTES - 6 * 1024 * 1024))


def _dot(a, b):
    return jnp.dot(a, b, preferred_element_type=F32)


def _layer_norm(x, g, b):
    mu = jnp.mean(x, axis=-1, keepdims=True)
    xc = x - mu
    var = jnp.mean(xc * xc, axis=-1, keepdims=True)
    return xc * lax.rsqrt(var + EPS) * g + b


def _softplus(x):
    return jnp.maximum(x, 0.0) + jnp.log1p(jnp.exp(-jnp.abs(x)))


def _gelu_tanh(x):
    c = math.sqrt(2.0 / math.pi)
    return 0.5 * x * (1.0 + jnp.tanh(c * (x + 0.044715 * (x * x * x))))


def _pick_tile(n, target, multiple):
    t = min(n, target)
    while t > multiple and (n % t or t % multiple):
        t -= multiple
    assert n % t == 0 and t % multiple == 0, (n, target, multiple)
    return t


def _resident(shape):
    return pl.BlockSpec(shape, lambda *_: (0,) * len(shape), pipeline_mode=pl.Buffered(1))


def _ln_cast_kernel(y_ref, g_ref, b_ref, o_ref):
    o_ref[...] = _layer_norm(y_ref[...], g_ref[...], b_ref[...]).astype(o_ref.dtype)


def _ln_cast_call(y, g, b):
    t, d = y.shape
    tm = _pick_tile(t, 512, BF16_ROWS)
    return pl.pallas_call(
        _ln_cast_kernel,
        grid=(t // tm,),
        in_specs=[pl.BlockSpec((tm, d), lambda i: (i, 0)), _resident((1, d)), _resident((1, d))],
        out_specs=pl.BlockSpec((tm, d), lambda i: (i, 0)),
        out_shape=jax.ShapeDtypeStruct((t, d), BF16),
        compiler_params=pltpu.CompilerParams(
            dimension_semantics=("parallel",),
            vmem_limit_bytes=_vmem_limit(6 * tm * d * 4)),
        name="ln_cast",
    )(y, g, b)


def _matmul_in_kernel(x_ref, w_ref, bias_ref, wg_ref, bg_ref, wg2_ref, bg2_ref, z_ref, gz_ref):
    @pl.when(pl.program_id(1) == 0)
    def _():
        zg = _dot(x_ref[...], wg_ref[...]) + bg_ref[...]
        gz_ref[...] = _dot(zg.astype(BF16), wg2_ref[...]) + bg2_ref[...]

    z_ref[...] = _dot(x_ref[...], w_ref[...]) + bias_ref[...]


def _matmul_in_call(xn, w, bias, wg, bg, wg2, bg2):
    t, d = xn.shape
    n = w.shape[1]
    ng = wg2.shape[1]
    tm = _pick_tile(t, 1024, BF16_ROWS)
    tn = _pick_tile(n, 1536, LANES)
    est = (2 * tm * d * 2 + 2 * d * tn * 2 + 3 * tm * tn * 4 + d * LANES * 2 + tm * LANES * 4
           + 3 * tm * ng * 4)
    return pl.pallas_call(
        _matmul_in_kernel,
        grid=(t // tm, n // tn),
        in_specs=[pl.BlockSpec((tm, d), lambda i, j: (i, 0)),
                  pl.BlockSpec((d, tn), lambda i, j: (0, j)),
                  pl.BlockSpec((1, tn), lambda i, j: (0, j)),
                  _resident((d, LANES)),
                  _resident((1, LANES)),
                  _resident((LANES, ng)),
                  _resident((1, ng))],
        out_specs=[pl.BlockSpec((tm, tn), lambda i, j: (i, j)),
                   pl.BlockSpec((tm, ng), lambda i, j: (i, 0))],
        out_shape=[jax.ShapeDtypeStruct((t, n), F32),
                   jax.ShapeDtypeStruct((t, ng), F32)],
        compiler_params=pltpu.CompilerParams(
            dimension_semantics=("parallel", "arbitrary"),
            vmem_limit_bytes=_vmem_limit(est)),
        name="matmul_in",
    )(xn, w, bias, wg, bg, wg2, bg2)


def _group_scan(a, u, reverse):
    rows, c = a.shape
    groups = rows // SUBLANES
    a3 = a.reshape(groups, SUBLANES, c)
    u3 = u.reshape(groups, SUBLANES, c)
    row = lax.broadcasted_iota(jnp.int32, (groups, SUBLANES, c), 1)
    for d in (1, 2, 4):
        shift = SUBLANES - d if reverse else d
        valid = (row < SUBLANES - d) if reverse else (row >= d)
        a_s = pltpu.roll(a3, shift, 1)
        u_s = pltpu.roll(u3, shift, 1)
        u3 = jnp.where(valid, a3 * u_s + u3, u3)
        a3 = jnp.where(valid, a3 * a_s, a3)
    return a3.reshape(rows, c), u3.reshape(rows, c)


def _rglru_kernel(zrx_ref, zrg_ref, cw_ref, cb_ref, wg_ref, bg_ref, lam_ref, o_ref,
                  xpad_ref, af_ref, uf_ref, ab_ref, ub_ref, *, tile):
    s, c = zrx_ref.shape
    n_tiles = s // tile
    n_groups = s // SUBLANES
    pad = SUBLANES

    xpad_ref[0:pad, :] = jnp.zeros((pad, c), F32)
    xpad_ref[pad + s:pad + s + pad, :] = jnp.zeros((pad, c), F32)

    def copy_body(i, _):
        t0 = pl.multiple_of(i * tile, tile)
        xpad_ref[pl.ds(pad + t0, tile), :] = zrx_ref[pl.ds(t0, tile), :]
        return 0

    lax.fori_loop(0, n_tiles, copy_body, 0)

    cw = cw_ref[...]
    cb = cb_ref[...]
    bg = bg_ref[...]
    decay = -RG_C * _softplus(-lam_ref[...])

    def local_body(i, _):
        t0 = pl.multiple_of(i * tile, tile)
        xa = cb
        for j in range(4):
            xa = xa + cw[j:j + 1, :] * xpad_ref[pl.ds(pad + t0 - 2 + j, tile), :]
        gz = _dot(xa.astype(BF16), wg_ref[...]) + bg
        for direction, (a_ref, u_ref) in enumerate(((af_ref, uf_ref), (ab_ref, ub_ref))):
            lo = 2 * c * direction
            r = jax.nn.sigmoid(gz[:, lo:lo + c])
            ig = jax.nn.sigmoid(gz[:, lo + c:lo + 2 * c])
            log_a = r * decay[:, c * direction:c * (direction + 1)]
            a = jnp.exp(log_a)
            th = jnp.tanh(log_a)
            one_minus_a2 = (-2.0 * th) / (1.0 - th)
            root = jnp.where(one_minus_a2 > 0.0, one_minus_a2 * lax.rsqrt(one_minus_a2), 0.0)
            u = root * (ig * xa)
            a_run, u_run = _group_scan(a, u, reverse=bool(direction))
            a_ref[pl.ds(t0, tile), :] = a_run
            u_ref[pl.ds(t0, tile), :] = u_run
        return 0

    lax.fori_loop(0, n_tiles, local_body, 0)

    def chain_body(g, carry):
        h_f, h_b = carry
        rf = pl.multiple_of(g * SUBLANES, SUBLANES)
        rb = pl.multiple_of((n_groups - 1 - g) * SUBLANES, SUBLANES)
        hf = af_ref[pl.ds(rf, SUBLANES), :] * h_f + uf_ref[pl.ds(rf, SUBLANES), :]
        uf_ref[pl.ds(rf, SUBLANES), :] = hf
        hb = ab_ref[pl.ds(rb, SUBLANES), :] * h_b + ub_ref[pl.ds(rb, SUBLANES), :]
        ub_ref[pl.ds(rb, SUBLANES), :] = hb
        return hf[SUBLANES - 1:SUBLANES], hb[0:1]

    zero = jnp.zeros((1, c), F32)
    lax.fori_loop(0, n_groups, chain_body, (zero, zero), unroll=8)

    def out_body(i, _):
        t0 = pl.multiple_of(i * tile, tile)
        h = uf_ref[pl.ds(t0, tile), :] + ub_ref[pl.ds(t0, tile), :]
        o_ref[pl.ds(t0, tile), :] = (_gelu_tanh(zrg_ref[pl.ds(t0, tile), :]) * h).astype(o_ref.dtype)
        return 0

    lax.fori_loop(0, n_tiles, out_body, 0)


def _rglru_call(z3, cw, cb, wg, bg, lam, d_rnn):
    b, s, _ = z3.shape
    c = RG_GROUP
    n_groups = d_rnn // c
    tile = _pick_tile(s, 256, SUBLANES)
    est = (2 * 2 * s * c * 4 + 2 * s * c * 2 + 5 * s * c * 4 + 2 * SUBLANES * c * 4
           + 40 * tile * c * 4 + 4 * c * 4 * c * 2)
    return pl.pallas_call(
        functools.partial(_rglru_kernel, tile=tile),
        grid=(b, n_groups),
        in_specs=[pl.BlockSpec((None, s, c), lambda i, j: (i, 0, j)),
                  pl.BlockSpec((None, s, c), lambda i, j: (i, 0, n_groups + j)),
                  pl.BlockSpec((None, 4, c), lambda i, j: (j, 0, 0)),
                  pl.BlockSpec((None, 1, c), lambda i, j: (j, 0, 0)),
                  pl.BlockSpec((None, c, 4 * c), lambda i, j: (j, 0, 0)),
                  pl.BlockSpec((None, 1, 4 * c), lambda i, j: (j, 0, 0)),
                  pl.BlockSpec((None, 1, 2 * c), lambda i, j: (j, 0, 0))],
        out_specs=pl.BlockSpec((None, s, c), lambda i, j: (i, 0, j)),
        out_shape=jax.ShapeDtypeStruct((b, s, d_rnn), BF16),
        scratch_shapes=[pltpu.VMEM((s + 2 * SUBLANES, c), F32)] + [pltpu.VMEM((s, c), F32)] * 4,
        compiler_params=pltpu.CompilerParams(
            dimension_semantics=("parallel", "parallel"),
            vmem_limit_bytes=_vmem_limit(est)),
        name="rglru",
    )(z3, z3, cw, cb, wg, bg, lam)


def _chunk_cumsum(g, reverse):
    rows, c = g.shape
    groups = rows // SUBLANES
    per_chunk = CHUNK // SUBLANES
    x = g.reshape(groups, SUBLANES, c)
    row = lax.broadcasted_iota(jnp.int32, (groups, SUBLANES, c), 1)
    for d in (1, 2, 4):
        shift = SUBLANES - d if reverse else d
        valid = (row < SUBLANES - d) if reverse else (row >= d)
        x = x + jnp.where(valid, pltpu.roll(x, shift, 1), 0.0)
    outs = [None] * groups
    for ch in range(groups // per_chunk):
        carry = None
        order = range(per_chunk - 1, -1, -1) if reverse else range(per_chunk)
        for gi in order:
            idx = ch * per_chunk + gi
            xg = x[idx] if carry is None else x[idx] + carry
            outs[idx] = xg
            carry = xg[0:1] if reverse else xg[SUBLANES - 1:SUBLANES]
    return jnp.concatenate(outs, axis=0)


def _gla_kernel(q_ref, k_ref, v_ref, gz_ref, o_ref, st_ref, *, reverse, scale):
    rows, dk = q_ref.shape
    pair = 2 * CHUNK
    n_chunks = rows // CHUNK
    assert dk == LANES and rows % pair == 0 and n_chunks <= LANES

    @pl.when(pl.program_id(2) == 0)
    def _():
        st_ref[...] = jnp.zeros_like(st_ref)

    gz = gz_ref[...]
    g = (jnp.minimum(gz, 0.0) - jnp.log(1.0 + jnp.exp(-jnp.abs(gz)))) * (1.0 / GATE_NORM)
    gc = _chunk_cumsum(g, reverse)
    g_last = [gc[c * CHUNK:c * CHUNK + 1] if reverse else gc[(c + 1) * CHUNK - 1:(c + 1) * CHUNK]
              for c in range(n_chunks)]
    g_last_rows = jnp.concatenate([jnp.broadcast_to(r, (CHUNK, dk)) for r in g_last], axis=0)
    q = q_ref[...] * scale
    k = k_ref[...]
    qd = (q * jnp.exp(gc)).astype(BF16)
    kd = (k * jnp.exp(-gc)).astype(BF16)
    k_end = (k * jnp.exp(g_last_rows - gc)).astype(BF16)
    dec_t = jnp.exp(jnp.concatenate(g_last + [jnp.zeros((LANES - n_chunks, dk), F32)], axis=0).T)

    ti = lax.broadcasted_iota(jnp.int32, (CHUNK, pair), 0)
    si = lax.broadcasted_iota(jnp.int32, (CHUNK, pair), 1)
    masks = []
    for half in range(2):
        local = si - half * CHUNK
        causal = (local >= ti) if reverse else (local <= ti)
        masks.append((local >= 0) & (local < CHUNK) & causal)

    order = list(range(n_chunks - 1, -1, -1) if reverse else range(n_chunks))
    v_pairs = [v_ref[p * pair:(p + 1) * pair, :].astype(BF16) for p in range(n_chunks // 2)]

    atts, kvs = {}, {}
    for ch in order:
        lo, half, p = ch * CHUNK, ch % 2, ch // 2
        att = lax.dot_general(qd[lo:lo + CHUNK], kd[p * pair:(p + 1) * pair], (((1,), (1,)), ((), ())),
                              preferred_element_type=F32)
        atts[ch] = jnp.where(masks[half], att, 0.0).astype(BF16)
        kvs[ch] = lax.dot_general(k_end[lo:lo + CHUNK], v_pairs[p][half * CHUNK:(half + 1) * CHUNK],
                                  (((0,), (0,)), ((), ())), preferred_element_type=F32)

    st = st_ref[...]
    st_in = {}
    for ch in order:
        st_in[ch] = st.astype(BF16)
        st = dec_t[:, ch:ch + 1] * st + kvs[ch]
    st_ref[...] = st

    for ch in order:
        lo = ch * CHUNK
        o_ref[lo:lo + CHUNK, :] = _dot(jnp.concatenate([qd[lo:lo + CHUNK], atts[ch]], axis=1),
                                       jnp.concatenate([st_in[ch], v_pairs[ch // 2]], axis=0))


def _gla_call(z3, gz3, *, reverse, col_q, col_k, col_v, dk, dv):
    b, s, _ = z3.shape
    rows = _pick_tile(s, 1024, 2 * CHUNK)
    nb = s // rows
    col_g = N_HEADS * dk if reverse else 0

    def blk(j):
        return nb - 1 - j if reverse else j

    est = 2 * rows * (3 * dk + 2 * dv) * 4 + 40 * rows * dk * 4 + 24 * dv * dk * 4
    return pl.pallas_call(
        functools.partial(_gla_kernel, reverse=reverse, scale=dk ** -0.5),
        grid=(b, N_HEADS, nb),
        in_specs=[pl.BlockSpec((None, rows, dk), lambda i, h, j: (i, blk(j), col_q // dk + h)),
                  pl.BlockSpec((None, rows, dk), lambda i, h, j: (i, blk(j), col_k // dk + h)),
                  pl.BlockSpec((None, rows, dv), lambda i, h, j: (i, blk(j), col_v // dv + h)),
                  pl.BlockSpec((None, rows, dk), lambda i, h, j: (i, blk(j), col_g // dk + h))],
        out_specs=pl.BlockSpec((None, rows, dv), lambda i, h, j: (i, blk(j), h)),
        out_shape=jax.ShapeDtypeStruct((b, s, N_HEADS * dv), F32),
        scratch_shapes=[pltpu.VMEM((dk, dv), F32)],
        compiler_params=pltpu.CompilerParams(
            dimension_semantics=("parallel", "parallel", "arbitrary"),
            vmem_limit_bytes=_vmem_limit(est)),
        name="gla_bwd" if reverse else "gla_fwd",
    )(z3, z3, z3, gz3)


def _mix_merge_kernel(ya_ref, of_ref, ob_ref, og_ref, nw_ref, ma_ref, mb_ref, y_ref, g_ref, b_ref,
                      wa_ref, wb_ref, wo_ref, g2_ref, b2_ref, o_ref, xn_ref, *, alpha, dv):
    o = of_ref[...] + ob_ref[...]
    parts = []
    for h in range(o.shape[1] // dv):
        oh = o[:, h * dv:(h + 1) * dv]
        parts.append(oh * lax.rsqrt(jnp.mean(oh * oh, axis=-1, keepdims=True) + EPS))
    og = og_ref[...]
    yb = (jnp.concatenate(parts, axis=1) * nw_ref[...] * (og * jax.nn.sigmoid(og))).astype(BF16)
    y_a = _dot(ya_ref[...], wa_ref[...])
    y_b = _dot(yb, wb_ref[...])
    merged = jax.nn.sigmoid(ma_ref[...]) * y_a + jax.nn.sigmoid(mb_ref[...]) * y_b
    mix = _dot(merged.astype(BF16), wo_ref[...])
    y = alpha * _layer_norm(y_ref[...], g_ref[...], b_ref[...]) + mix
    o_ref[...] = y
    xn_ref[...] = _layer_norm(y, g2_ref[...], b2_ref[...]).astype(xn_ref.dtype)


def _mix_merge_call(ya, o_f, o_b, z, nw, y, ln, wa, wb, wo, ln_next, *, col_og, col_ma, col_mb,
                    alpha, dv):
    t, d = y.shape
    dr = ya.shape[1]
    tm = _pick_tile(t, 256, BF16_ROWS)
    est = (2 * tm * dr * 2 + 2 * 3 * tm * dr * 4 + 2 * 4 * tm * d * 4 + 2 * tm * d * 2
           + (2 * dr * d + d * d) * 2 + 8 * tm * d * 4)
    row = lambda i: (i, 0)
    return pl.pallas_call(
        functools.partial(_mix_merge_kernel, alpha=alpha, dv=dv),
        grid=(t // tm,),
        in_specs=[pl.BlockSpec((tm, dr), row),
                  pl.BlockSpec((tm, dr), row),
                  pl.BlockSpec((tm, dr), row),
                  pl.BlockSpec((tm, dr), lambda i: (i, col_og // dr)),
                  _resident((1, dr)),
                  pl.BlockSpec((tm, d), lambda i: (i, col_ma // d)),
                  pl.BlockSpec((tm, d), lambda i: (i, col_mb // d)),
                  pl.BlockSpec((tm, d), row),
                  _resident((1, d)), _resident((1, d)),
                  _resident((dr, d)), _resident((dr, d)), _resident((d, d)),
                  _resident((1, d)), _resident((1, d))],
        out_specs=[pl.BlockSpec((tm, d), row), pl.BlockSpec((tm, d), row)],
        out_shape=[jax.ShapeDtypeStruct((t, d), F32), jax.ShapeDtypeStruct((t, d), BF16)],
        compiler_params=pltpu.CompilerParams(
            dimension_semantics=("parallel",),
            vmem_limit_bytes=_vmem_limit(est)),
        name="mix_merge",
    )(ya, o_f, o_b, z, nw, z, z, y, ln[0], ln[1], wa, wb, wo, ln_next[0], ln_next[1])


def _ffn_up_kernel(x_ref, xp_ref, xq_ref, wg_ref, wv_ref, cw_ref, cb_ref, o_ref, xs_ref,
                   *, tiles_per_seq):
    tm = x_ref.shape[0]
    halo = xp_ref.shape[0]
    i = pl.program_id(0)

    @pl.when(pl.program_id(1) == 0)
    def _():
        xs_ref[halo:halo + tm, :] = x_ref[...]
        has_prev = (i % tiles_per_seq) != 0
        has_next = (i % tiles_per_seq) != tiles_per_seq - 1
        xs_ref[0:halo, :] = jnp.where(has_prev, xp_ref[...], jnp.zeros_like(xp_ref))
        xs_ref[halo + tm:halo + tm + halo, :] = jnp.where(has_next, xq_ref[...], jnp.zeros_like(xq_ref))

    ug = _dot(xs_ref[...], wg_ref[...])
    uv = _dot(x_ref[...], wv_ref[...])
    n_rows = ug.shape[0]
    cw = cw_ref[...]
    conv = (cb_ref[...] + cw[0:1, :] * pltpu.roll(ug, 1, 0) + cw[1:2, :] * ug
            + cw[2:3, :] * pltpu.roll(ug, n_rows - 1, 0))
    conv = conv[halo:halo + tm, :]
    o_ref[...] = (_gelu_tanh(conv) * uv).astype(o_ref.dtype)


def _ffn_up_call(xn, w_up, cw, cb, seq_len):
    t, d = xn.shape
    d_ff = w_up.shape[1] // 2
    halo = BF16_ROWS
    tm = _pick_tile(seq_len, 1024, halo)
    tn = _pick_tile(d_ff, 768, LANES)
    tiles_per_seq = seq_len // tm
    hb = tm // halo
    n_hblocks = t // halo
    n_j = d_ff // tn
    est = (2 * tm * d * 2 + 4 * halo * d * 2 + (tm + 2 * halo) * d * 2 + 2 * 2 * d * tn * 2
           + 2 * tm * tn * 2 + 7 * (tm + 2 * halo) * tn * 4)
    return pl.pallas_call(
        functools.partial(_ffn_up_kernel, tiles_per_seq=tiles_per_seq),
        grid=(t // tm, n_j),
        in_specs=[pl.BlockSpec((tm, d), lambda i, j: (i, 0)),
                  pl.BlockSpec((halo, d), lambda i, j: (jnp.maximum(i * hb - 1, 0), 0)),
                  pl.BlockSpec((halo, d), lambda i, j: (jnp.minimum((i + 1) * hb, n_hblocks - 1), 0)),
                  pl.BlockSpec((d, tn), lambda i, j: (0, j)),
                  pl.BlockSpec((d, tn), lambda i, j: (0, n_j + j)),
                  pl.BlockSpec((3, tn), lambda i, j: (0, j)),
                  pl.BlockSpec((1, tn), lambda i, j: (0, j))],
        out_specs=pl.BlockSpec((tm, tn), lambda i, j: (i, j)),
        out_shape=jax.ShapeDtypeStruct((t, d_ff), BF16),
        scratch_shapes=[pltpu.VMEM((tm + 2 * halo, d), BF16)],
        compiler_params=pltpu.CompilerParams(
            dimension_semantics=("parallel", "arbitrary"),
            vmem_limit_bytes=_vmem_limit(est)),
        name="ffn_up",
    )(xn, xn, xn, w_up, w_up, cw, cb)


def _ffn_down_kernel(h_ref, w_ref, y_ref, g_ref, b_ref, g2_ref, b2_ref, *out_refs, alpha, last):
    y = alpha * _layer_norm(y_ref[...], g_ref[...], b_ref[...]) + _dot(h_ref[...], w_ref[...])
    xn = _layer_norm(y, g2_ref[...], b2_ref[...])
    if last:
        out_refs[0][...] = xn
    else:
        out_refs[0][...] = y
        out_refs[1][...] = xn.astype(out_refs[1].dtype)


def _ffn_down_call(h, w, y, ln, ln_next, *, alpha, last):
    t, d = y.shape
    d_ff = h.shape[1]
    tm = _pick_tile(t, 256, BF16_ROWS)
    est = d_ff * d * 2 + 2 * tm * d_ff * 2 + 4 * tm * d * 4 + 2 * tm * d * 2 + 6 * tm * d * 4
    row = lambda i: (i, 0)
    out_specs = [pl.BlockSpec((tm, d), row)]
    out_shape = [jax.ShapeDtypeStruct((t, d), F32)]
    if not last:
        out_specs.append(pl.BlockSpec((tm, d), row))
        out_shape.append(jax.ShapeDtypeStruct((t, d), BF16))
    return pl.pallas_call(
        functools.partial(_ffn_down_kernel, alpha=alpha, last=last),
        grid=(t // tm,),
        in_specs=[pl.BlockSpec((tm, d_ff), row),
                  _resident((d_ff, d)),
                  pl.BlockSpec((tm, d), row),
                  _resident((1, d)), _resident((1, d)), _resident((1, d)), _resident((1, d))],
        out_specs=out_specs,
        out_shape=out_shape,
        compiler_params=pltpu.CompilerParams(
            dimension_semantics=("parallel",),
            vmem_limit_bytes=_vmem_limit(est)),
        name="ffn_down",
    )(h, w, y, ln[0], ln[1], ln_next[0], ln_next[1])


def _prepare_layer(p, l, d_model):
    d_rnn = p["conv_rg_w"].shape[-1]
    dk_tot = p["gla_wg2"].shape[-1]
    rank = p["gla_wg2"].shape[-2]
    dv_tot = p["w_proj_b"].shape[1]
    block_w = p["rg_wa"].shape[3]
    dk, dv = dk_tot // N_HEADS, dv_tot // N_HEADS

    sizes = [d_rnn, d_rnn, dk_tot, dk_tot, dv_tot, dv_tot, rank, rank, d_model, d_model]
    offs = [0]
    for sz in sizes:
        offs.append(offs[-1] + sz)
    rx, rg, q, k, v, og, gf, gb, ma, mb = [slice(offs[i], offs[i + 1]) for i in range(10)]
    order = [rx, rg, ma, mb, q, k, v, og]
    w_in, b_in = p["w_in"][l], p["b_in"][l]
    w_main = jnp.concatenate([w_in[:, s] for s in order], axis=1).astype(BF16)
    b_main = jnp.concatenate([b_in[s] for s in order])[None, :]
    cols, c0 = {}, 0
    for name, s in zip(("rx", "rg", "ma", "mb", "q", "k", "v", "og"), order):
        cols[name] = c0
        c0 += s.stop - s.start
    w_gate = jnp.zeros((d_model, LANES), F32).at[:, :2 * rank].set(w_in[:, gf.start:gb.stop]).astype(BF16)
    b_gate = jnp.zeros((1, LANES), F32).at[0, :2 * rank].set(b_in[gf.start:gb.stop])

    c = RG_GROUP
    n_groups = d_rnn // c
    per = c // block_w

    def group_dense(w):
        wgrp = w.reshape(n_groups, per, block_w, block_w)
        eye = jnp.eye(per, dtype=w.dtype)
        return jnp.einsum("gpij,pq->gpiqj", wgrp, eye).reshape(n_groups, c, c)

    rg_w = jnp.concatenate([group_dense(p["rg_wa"][l, 0]), group_dense(p["rg_wx"][l, 0]),
                            group_dense(p["rg_wa"][l, 1]), group_dense(p["rg_wx"][l, 1])],
                           axis=2).astype(BF16)

    def grp(vec):
        return vec.reshape(n_groups, 1, c)

    rg_b = jnp.concatenate([grp(p["rg_ba"][l, 0]), grp(p["rg_bx"][l, 0]),
                            grp(p["rg_ba"][l, 1]), grp(p["rg_bx"][l, 1])], axis=2)
    rg_lam = jnp.concatenate([grp(p["rg_lam"][l, 0]), grp(p["rg_lam"][l, 1])], axis=2)
    conv_w = p["conv_rg_w"][l].reshape(-1, n_groups, c).transpose(1, 0, 2)
    conv_b = grp(p["conv_rg_b"][l])

    w_gate2 = jnp.zeros((LANES, 2 * dk_tot), F32)
    w_gate2 = w_gate2.at[:rank, :dk_tot].set(p["gla_wg2"][l, 0])
    w_gate2 = w_gate2.at[rank:2 * rank, dk_tot:].set(p["gla_wg2"][l, 1]).astype(BF16)
    b_gate2 = jnp.concatenate([p["gla_bg"][l, 0], p["gla_bg"][l, 1]])[None, :]

    return dict(
        cols=cols, d_rnn=d_rnn, dk=dk, dv=dv,
        w_main=w_main, b_main=b_main, w_gate=w_gate, b_gate=b_gate,
        rg_w=rg_w, rg_b=rg_b, rg_lam=rg_lam, conv_w=conv_w, conv_b=conv_b,
        w_gate2=w_gate2, b_gate2=b_gate2,
        norm_w=jnp.tile(p["gla_norm_w"][l], N_HEADS)[None, :],
        w_pa=p["w_proj_a"][l].astype(BF16), w_pb=p["w_proj_b"][l].astype(BF16),
        w_out=p["w_out"][l].astype(BF16),
        ln_mix=(p["ln_mix_g"][l][None, :], p["ln_mix_b"][l][None, :]),
        w_up=p["w_up"][l].astype(BF16), conv_ff_w=p["conv_ff_w"][l],
        conv_ff_b=p["conv_ff_b"][l][None, :], w_down=p["w_down"][l].astype(BF16),
        ln_ffn=(p["ln_ffn_g"][l][None, :], p["ln_ffn_b"][l][None, :]),
    )


def _trunk(x, ln_in, layers, alpha):
    b, s, d = x.shape
    y = x.reshape(b * s, d)
    ln = ln_in
    xn = _ln_cast_call(y, ln[0], ln[1])
    for li, lp in enumerate(layers):
        cols = lp["cols"]
        z, gz = _matmul_in_call(xn, lp["w_main"], lp["b_main"], lp["w_gate"], lp["b_gate"],
                                lp["w_gate2"], lp["b_gate2"])
        z3 = z.reshape(b, s, z.shape[1])
        gz3 = gz.reshape(b, s, gz.shape[1])
        ya = _rglru_call(z3, lp["conv_w"], lp["conv_b"], lp["rg_w"], lp["rg_b"], lp["rg_lam"],
                         lp["d_rnn"])
        gla = functools.partial(_gla_call, z3, gz3, col_q=cols["q"], col_k=cols["k"],
                                col_v=cols["v"], dk=lp["dk"], dv=lp["dv"])
        o_f = gla(reverse=False)
        o_b = gla(reverse=True)
        y_mix, xn_mix = _mix_merge_call(
            ya.reshape(b * s, -1), o_f.reshape(b * s, -1), o_b.reshape(b * s, -1), z, lp["norm_w"],
            y, ln, lp["w_pa"], lp["w_pb"], lp["w_out"], lp["ln_mix"],
            col_og=cols["og"], col_ma=cols["ma"], col_mb=cols["mb"], alpha=alpha, dv=lp["dv"])
        hdn = _ffn_up_call(xn_mix, lp["w_up"], lp["conv_ff_w"], lp["conv_ff_b"], s)
        last = li == len(layers) - 1
        outs = _ffn_down_call(hdn, lp["w_down"], y_mix, lp["ln_mix"], lp["ln_ffn"], alpha=alpha,
                              last=last)
        if last:
            return outs[0].reshape(b, s, d)
        y, xn = outs
        ln = lp["ln_ffn"]


def kernel(x_prompt, x_sample, ln_in_g, ln_in_b, w_in, b_in, conv_rg_w, conv_rg_b, rg_wa, rg_ba, rg_wx, rg_bx, rg_lam, gla_wg2, gla_bg, gla_norm_w, w_proj_a, w_proj_b, w_out, ln_mix_g, ln_mix_b, w_up, conv_ff_w, conv_ff_b, w_down, ln_ffn_g, ln_ffn_b):
    p = dict(w_in=w_in, b_in=b_in, conv_rg_w=conv_rg_w, conv_rg_b=conv_rg_b, rg_wa=rg_wa,
             rg_ba=rg_ba, rg_wx=rg_wx, rg_bx=rg_bx, rg_lam=rg_lam, gla_wg2=gla_wg2, gla_bg=gla_bg,
             gla_norm_w=gla_norm_w, w_proj_a=w_proj_a, w_proj_b=w_proj_b, w_out=w_out,
             ln_mix_g=ln_mix_g, ln_mix_b=ln_mix_b, w_up=w_up, conv_ff_w=conv_ff_w,
             conv_ff_b=conv_ff_b, w_down=w_down, ln_ffn_g=ln_ffn_g, ln_ffn_b=ln_ffn_b)
    depth = w_in.shape[0]
    d_model = w_in.shape[1]
    alpha = (2.0 * depth) ** 0.25
    layers = [_prepare_layer(p, l, d_model) for l in range(depth)]
    ln_in = (ln_in_g[None, :], ln_in_b[None, :])
    return (_trunk(x_prompt, ln_in, layers, alpha), _trunk(x_sample, ln_in, layers, alpha))
```

```python
import functools
import math

import jax
import jax.numpy as jnp
from jax import lax
from jax.experimental import pallas as pl
from jax.experimental.pallas import tpu as pltpu

F32 = jnp.float32
BF16 = jnp.bfloat16

V7X_VMEM_BYTES = 64 * 1024 * 1024
LANES = 128
SUBLANES = 8
BF16_ROWS = 16

N_HEADS = 4
CHUNK = 64
RG_C = 8.0
GATE_NORM = 16.0
EPS = 1e-5
RG_GROUP = 128
SCAN_BLOCK = SUBLANES * SUBLANES


def _vmem_limit(nbytes):
    return int(min(max(nbytes, 16 * 1024 * 1024), V7X_VMEM_BYTES - 6 * 1024 * 1024))


def _dot(a, b):
    return jnp.dot(a, b, preferred_element_type=F32)


def _layer_norm(x, g, b):
    mu = jnp.mean(x, axis=-1, keepdims=True)
    xc = x - mu
    var = jnp.mean(xc * xc, axis=-1, keepdims=True)
    return xc * lax.rsqrt(var + EPS) * g + b


def _softplus(x):
    return jnp.maximum(x, 0.0) + jnp.log1p(jnp.exp(-jnp.abs(x)))


def _gelu_tanh(x):
    c = math.sqrt(2.0 / math.pi)
    return 0.5 * x * (1.0 + jnp.tanh(c * (x + 0.044715 * (x * x * x))))


def _pick_tile(n, target, multiple):
    t = min(n, target)
    while t > multiple and (n % t or t % multiple):
        t -= multiple
    assert n % t == 0 and t % multiple == 0, (n, target, multiple)
    return t


def _resident(shape):
    return pl.BlockSpec(shape, lambda *_: (0,) * len(shape), pipeline_mode=pl.Buffered(1))


def _ln_cast_kernel(y_ref, g_ref, b_ref, o_ref):
    o_ref[...] = _layer_norm(y_ref[...], g_ref[...], b_ref[...]).astype(o_ref.dtype)


def _ln_cast_call(y, g, b):
    t, d = y.shape
    tm = _pick_tile(t, 512, BF16_ROWS)
    return pl.pallas_call(
        _ln_cast_kernel,
        grid=(t // tm,),
        in_specs=[pl.BlockSpec((tm, d), lambda i: (i, 0)), _resident((1, d)), _resident((1, d))],
        out_specs=pl.BlockSpec((tm, d), lambda i: (i, 0)),
        out_shape=jax.ShapeDtypeStruct((t, d), BF16),
        compiler_params=pltpu.CompilerParams(
            dimension_semantics=("parallel",),
            vmem_limit_bytes=_vmem_limit(6 * tm * d * 4)),
        name="ln_cast",
    )(y, g, b)


def _matmul_in_kernel(x_ref, w_ref, bias_ref, wg_ref, bg_ref, wg2_ref, bg2_ref, z_ref, gz_ref,
                      *, gate_rows):
    n_slices = x_ref.shape[0] // gate_rows
    r0 = pl.multiple_of(jnp.minimum(pl.program_id(1), n_slices - 1) * gate_rows, gate_rows)
    zg = _dot(x_ref[pl.ds(r0, gate_rows), :], wg_ref[...]) + bg_ref[...]
    z_ref[...] = _dot(x_ref[...], w_ref[...]) + bias_ref[...]
    gz_ref[pl.ds(r0, gate_rows), :] = _dot(zg.astype(BF16), wg2_ref[...]) + bg2_ref[...]


def _matmul_in_call(xn, w, bias, wg, bg, wg2, bg2):
    t, d = xn.shape
    n = w.shape[1]
    ng = wg2.shape[1]
    tm = _pick_tile(t, 1024, BF16_ROWS)
    tn = _pick_tile(n, 1536, 2 * LANES)
    gate_rows = _pick_tile(tm, tm, BF16_ROWS)
    while tm % (gate_rows // 2) == 0 and (gate_rows // 2) % BF16_ROWS == 0 \
            and (tm // (gate_rows // 2)) <= n // tn:
        gate_rows //= 2
    est = (2 * tm * d * 2 + 2 * d * tn * 2 + 3 * tm * tn * 4 + d * LANES * 2 + tm * LANES * 4
           + 3 * tm * ng * 4)
    return pl.pallas_call(
        functools.partial(_matmul_in_kernel, gate_rows=gate_rows),
        grid=(t // tm, n // tn),
        in_specs=[pl.BlockSpec((tm, d), lambda i, j: (i, 0)),
                  pl.BlockSpec((d, tn), lambda i, j: (0, j)),
                  pl.BlockSpec((1, tn), lambda i, j: (0, j)),
                  _resident((d, LANES)),
                  _resident((1, LANES)),
                  _resident((LANES, ng)),
                  _resident((1, ng))],
        out_specs=[pl.BlockSpec((tm, tn), lambda i, j: (i, j)),
                   pl.BlockSpec((tm, ng), lambda i, j: (i, 0))],
        out_shape=[jax.ShapeDtypeStruct((t, n), F32),
                   jax.ShapeDtypeStruct((t, ng), F32)],
        compiler_params=pltpu.CompilerParams(
            dimension_semantics=("parallel", "arbitrary"),
            vmem_limit_bytes=_vmem_limit(est)),
        name="matmul_in",
    )(xn, w, bias, wg, bg, wg2, bg2)


def _block_transpose(x):
    rows, c = x.shape
    blocks = rows // SCAN_BLOCK
    return jnp.swapaxes(x.reshape(blocks, SUBLANES, SUBLANES, c), 1, 2).reshape(rows, c)


def _block_scan(a, u, reverse):
    rows, c = a.shape
    blocks = rows // SCAN_BLOCK
    at = a.reshape(blocks, SUBLANES, SUBLANES, c)
    ut = u.reshape(blocks, SUBLANES, SUBLANES, c)
    order = list(range(SUBLANES - 1, -1, -1) if reverse else range(SUBLANES))
    p, h = {}, {}
    prev = None
    for j in order:
        if prev is None:
            p[j], h[j] = at[:, j], ut[:, j]
        else:
            p[j], h[j] = at[:, j] * p[prev], at[:, j] * h[prev] + ut[:, j]
        prev = j
    q, e = p[prev], h[prev]
    row = lax.broadcasted_iota(jnp.int32, q.shape, 1)
    for d in (1, 2, 4):
        shift = SUBLANES - d if reverse else d
        valid = (row < SUBLANES - d) if reverse else (row >= d)
        q_s = pltpu.roll(q, shift, 1)
        e_s = pltpu.roll(e, shift, 1)
        e = jnp.where(valid, q * e_s + e, e)
        q = jnp.where(valid, q * q_s, q)
    shift = SUBLANES - 1 if reverse else 1
    first = (row == SUBLANES - 1) if reverse else (row == 0)
    q_in = jnp.where(first, 1.0, pltpu.roll(q, shift, 1))
    e_in = jnp.where(first, 0.0, pltpu.roll(e, shift, 1))
    a_out = jnp.stack([p[j] * q_in for j in range(SUBLANES)], axis=1)
    u_out = jnp.stack([h[j] + p[j] * e_in for j in range(SUBLANES)], axis=1)
    return a_out.reshape(rows, c), u_out.reshape(rows, c)


def _rglru_kernel(zrx_ref, zrg_ref, cw_ref, cb_ref, wg_ref, bg_ref, lam_ref, o_ref,
                  xpad_ref, af_ref, uf_ref, ab_ref, ub_ref, *, tile):
    s, c = zrx_ref.shape
    n_tiles = s // tile
    n_groups = s // SCAN_BLOCK
    pad = SUBLANES

    xpad_ref[0:pad, :] = jnp.zeros((pad, c), F32)
    xpad_ref[pad + s:pad + s + pad, :] = jnp.zeros((pad, c), F32)

    def copy_body(i, _):
        t0 = pl.multiple_of(i * tile, tile)
        xpad_ref[pl.ds(pad + t0, tile), :] = zrx_ref[pl.ds(t0, tile), :]
        return 0

    lax.fori_loop(0, n_tiles, copy_body, 0)

    cw = cw_ref[...]
    cb = cb_ref[...]
    bg = bg_ref[...]
    decay = -RG_C * _softplus(-lam_ref[...])

    def local_body(i, _):
        t0 = pl.multiple_of(i * tile, tile)
        xa = cb
        for j in range(4):
            xa = xa + cw[j:j + 1, :] * xpad_ref[pl.ds(pad + t0 - 2 + j, tile), :]
        xa = _block_transpose(xa)
        gz = _dot(xa.astype(BF16), wg_ref[...]) + bg
        for direction, (a_ref, u_ref) in enumerate(((af_ref, uf_ref), (ab_ref, ub_ref))):
            lo = 2 * c * direction
            r = jax.nn.sigmoid(gz[:, lo:lo + c])
            ig = jax.nn.sigmoid(gz[:, lo + c:lo + 2 * c])
            log_a = r * decay[:, c * direction:c * (direction + 1)]
            a = jnp.exp(log_a)
            th = jnp.tanh(log_a)
            one_minus_a2 = (-2.0 * th) / (1.0 - th)
            root = jnp.where(one_minus_a2 > 0.0, one_minus_a2 * lax.rsqrt(one_minus_a2), 0.0)
            u = root * (ig * xa)
            a_run, u_run = _block_scan(a, u, reverse=bool(direction))
            a_ref[pl.ds(t0, tile), :] = a_run
            u_ref[pl.ds(t0, tile), :] = u_run
        return 0

    lax.fori_loop(0, n_tiles, local_body, 0)

    def chain_body(g, carry):
        h_f, h_b = carry
        rf = pl.multiple_of(g * SCAN_BLOCK, SCAN_BLOCK)
        rb = pl.multiple_of((n_groups - 1 - g) * SCAN_BLOCK, SCAN_BLOCK)
        hf = af_ref[pl.ds(rf, SCAN_BLOCK), :] * h_f + uf_ref[pl.ds(rf, SCAN_BLOCK), :]
        uf_ref[pl.ds(rf, SCAN_BLOCK), :] = hf
        hb = ab_ref[pl.ds(rb, SCAN_BLOCK), :] * h_b + ub_ref[pl.ds(rb, SCAN_BLOCK), :]
        ub_ref[pl.ds(rb, SCAN_BLOCK), :] = hb
        return hf[SCAN_BLOCK - 1:SCAN_BLOCK], hb[0:1]

    zero = jnp.zeros((1, c), F32)
    lax.fori_loop(0, n_groups, chain_body, (zero, zero), unroll=4)

    def out_body(i, _):
        t0 = pl.multiple_of(i * tile, tile)
        h = _block_transpose(uf_ref[pl.ds(t0, tile), :] + ub_ref[pl.ds(t0, tile), :])
        o_ref[pl.ds(t0, tile), :] = (_gelu_tanh(zrg_ref[pl.ds(t0, tile), :]) * h).astype(o_ref.dtype)
        return 0

    lax.fori_loop(0, n_tiles, out_body, 0)


def _rglru_call(z3, cw, cb, wg, bg, lam, d_rnn):
    b, s, _ = z3.shape
    c = RG_GROUP
    n_groups = d_rnn // c
    tile = _pick_tile(s, 512, SCAN_BLOCK)
    est = (2 * 2 * s * c * 4 + 2 * s * c * 2 + 5 * s * c * 4 + 2 * SUBLANES * c * 4
           + 40 * tile * c * 4 + 4 * c * 4 * c * 2)
    return pl.pallas_call(
        functools.partial(_rglru_kernel, tile=tile),
        grid=(b, n_groups),
        in_specs=[pl.BlockSpec((None, s, c), lambda i, j: (i, 0, j)),
                  pl.BlockSpec((None, s, c), lambda i, j: (i, 0, n_groups + j)),
                  pl.BlockSpec((None, 4, c), lambda i, j: (j, 0, 0)),
                  pl.BlockSpec((None, 1, c), lambda i, j: (j, 0, 0)),
                  pl.BlockSpec((None, c, 4 * c), lambda i, j: (j, 0, 0)),
                  pl.BlockSpec((None, 1, 4 * c), lambda i, j: (j, 0, 0)),
                  pl.BlockSpec((None, 1, 2 * c), lambda i, j: (j, 0, 0))],
        out_specs=pl.BlockSpec((None, s, c), lambda i, j: (i, 0, j)),
        out_shape=jax.ShapeDtypeStruct((b, s, d_rnn), BF16),
        scratch_shapes=[pltpu.VMEM((s + 2 * SUBLANES, c), F32)] + [pltpu.VMEM((s, c), F32)] * 4,
        compiler_params=pltpu.CompilerParams(
            dimension_semantics=("parallel", "parallel"),
            vmem_limit_bytes=_vmem_limit(est)),
        name="rglru",
    )(z3, z3, cw, cb, wg, bg, lam)


def _chunk_cumsum(g, reverse):
    rows, c = g.shape
    groups = rows // SUBLANES
    per_chunk = CHUNK // SUBLANES
    x = g.reshape(groups, SUBLANES, c)
    row = lax.broadcasted_iota(jnp.int32, (groups, SUBLANES, c), 1)
    for d in (1, 2, 4):
        shift = SUBLANES - d if reverse else d
        valid = (row < SUBLANES - d) if reverse else (row >= d)
        x = x + jnp.where(valid, pltpu.roll(x, shift, 1), 0.0)
    outs = [None] * groups
    for ch in range(groups // per_chunk):
        carry = None
        order = range(per_chunk - 1, -1, -1) if reverse else range(per_chunk)
        for gi in order:
            idx = ch * per_chunk + gi
            xg = x[idx] if carry is None else x[idx] + carry
            outs[idx] = xg
            carry = xg[0:1] if reverse else xg[SUBLANES - 1:SUBLANES]
    return jnp.concatenate(outs, axis=0)


def _gla_kernel(q_ref, k_ref, v_ref, gz_ref, o_ref, st_ref, *, reverse, scale):
    rows, dk = q_ref.shape
    pair = 2 * CHUNK
    n_chunks = rows // CHUNK
    assert dk == LANES and rows % pair == 0 and n_chunks <= LANES

    @pl.when(pl.program_id(2) == 0)
    def _():
        st_ref[...] = jnp.zeros_like(st_ref)

    gz = gz_ref[...]
    g = (jnp.minimum(gz, 0.0) - jnp.log(1.0 + jnp.exp(-jnp.abs(gz)))) * (1.0 / GATE_NORM)
    gc = _chunk_cumsum(g, reverse)
    g_last = [gc[c * CHUNK:c * CHUNK + 1] if reverse else gc[(c + 1) * CHUNK - 1:(c + 1) * CHUNK]
              for c in range(n_chunks)]
    g_last_rows = jnp.concatenate([jnp.broadcast_to(r, (CHUNK, dk)) for r in g_last], axis=0)
    q = q_ref[...] * scale
    k = k_ref[...]
    qd = (q * jnp.exp(gc)).astype(BF16)
    kd = (k * jnp.exp(-gc)).astype(BF16)
    k_end = (k * jnp.exp(g_last_rows - gc)).astype(BF16)
    dec_t = jnp.exp(jnp.concatenate(g_last + [jnp.zeros((LANES - n_chunks, dk), F32)], axis=0).T)

    ti = lax.broadcasted_iota(jnp.int32, (CHUNK, pair), 0)
    si = lax.broadcasted_iota(jnp.int32, (CHUNK, pair), 1)
    masks = []
    for half in range(2):
        local = si - half * CHUNK
        causal = (local >= ti) if reverse else (local <= ti)
        masks.append((local >= 0) & (local < CHUNK) & causal)

    order = list(range(n_chunks - 1, -1, -1) if reverse else range(n_chunks))
    v_pairs = [v_ref[p * pair:(p + 1) * pair, :].astype(BF16) for p in range(n_chunks // 2)]

    atts, kvs = {}, {}
    for ch in order:
        lo, half, p = ch * CHUNK, ch % 2, ch // 2
        att = lax.dot_general(qd[lo:lo + CHUNK], kd[p * pair:(p + 1) * pair], (((1,), (1,)), ((), ())),
                              preferred_element_type=F32)
        atts[ch] = jnp.where(masks[half], att, 0.0).astype(BF16)
        kvs[ch] = lax.dot_general(k_end[lo:lo + CHUNK], v_pairs[p][half * CHUNK:(half + 1) * CHUNK],
                                  (((0,), (0,)), ((), ())), preferred_element_type=F32)

    st = st_ref[...]
    st_in = {}
    for ch in order:
        st_in[ch] = st.astype(BF16)
        st = dec_t[:, ch:ch + 1] * st + kvs[ch]
    st_ref[...] = st

    for ch in order:
        lo = ch * CHUNK
        o_ref[lo:lo + CHUNK, :] = _dot(jnp.concatenate([qd[lo:lo + CHUNK], atts[ch]], axis=1),
                                       jnp.concatenate([st_in[ch], v_pairs[ch // 2]], axis=0))


def _gla_call(z3, gz3, *, reverse, col_q, col_k, col_v, dk, dv):
    b, s, _ = z3.shape
    rows = _pick_tile(s, 1024, 2 * CHUNK)
    nb = s // rows
    col_g = N_HEADS * dk if reverse else 0

    def blk(j):
        return nb - 1 - j if reverse else j

    est = 2 * rows * (3 * dk + 2 * dv) * 4 + 40 * rows * dk * 4 + 24 * dv * dk * 4
    return pl.pallas_call(
        functools.partial(_gla_kernel, reverse=reverse, scale=dk ** -0.5),
        grid=(b, N_HEADS, nb),
        in_specs=[pl.BlockSpec((None, rows, dk), lambda i, h, j: (i, blk(j), col_q // dk + h)),
                  pl.BlockSpec((None, rows, dk), lambda i, h, j: (i, blk(j), col_k // dk + h)),
                  pl.BlockSpec((None, rows, dv), lambda i, h, j: (i, blk(j), col_v // dv + h)),
                  pl.BlockSpec((None, rows, dk), lambda i, h, j: (i, blk(j), col_g // dk + h))],
        out_specs=pl.BlockSpec((None, rows, dv), lambda i, h, j: (i, blk(j), h)),
        out_shape=jax.ShapeDtypeStruct((b, s, N_HEADS * dv), F32),
        scratch_shapes=[pltpu.VMEM((dk, dv), F32)],
        compiler_params=pltpu.CompilerParams(
            dimension_semantics=("parallel", "parallel", "arbitrary"),
            vmem_limit_bytes=_vmem_limit(est)),
        name="gla_bwd" if reverse else "gla_fwd",
    )(z3, z3, z3, gz3)


def _branch_merge_kernel(ya_ref, of_ref, ob_ref, og_ref, nw_ref, ma_ref, mb_ref, wa_ref, wb_ref,
                         o_ref, *, dv):
    y_a = _dot(ya_ref[...], wa_ref[...])
    o = of_ref[...] + ob_ref[...]
    parts = []
    for h in range(o.shape[1] // dv):
        oh = o[:, h * dv:(h + 1) * dv]
        parts.append(oh * lax.rsqrt(jnp.mean(oh * oh, axis=-1, keepdims=True) + EPS))
    og = og_ref[...]
    yb = (jnp.concatenate(parts, axis=1) * nw_ref[...] * (og * jax.nn.sigmoid(og))).astype(BF16)
    y_b = _dot(yb, wb_ref[...])
    merged = jax.nn.sigmoid(ma_ref[...]) * y_a + jax.nn.sigmoid(mb_ref[...]) * y_b
    o_ref[...] = merged.astype(o_ref.dtype)


def _branch_merge_call(ya, o_f, o_b, z, nw, wa, wb, *, col_og, col_ma, col_mb, dv):
    t, dr = ya.shape
    d = wa.shape[1]
    tm = _pick_tile(t, 512, BF16_ROWS)
    est = (2 * tm * dr * 2 + 2 * 3 * tm * dr * 4 + 2 * 2 * tm * d * 4 + 2 * tm * d * 2
           + 2 * dr * d * 2 + 5 * tm * d * 4)
    row = lambda i: (i, 0)
    return pl.pallas_call(
        functools.partial(_branch_merge_kernel, dv=dv),
        grid=(t // tm,),
        in_specs=[pl.BlockSpec((tm, dr), row),
                  pl.BlockSpec((tm, dr), row),
                  pl.BlockSpec((tm, dr), row),
                  pl.BlockSpec((tm, dr), lambda i: (i, col_og // dr)),
                  _resident((1, dr)),
                  pl.BlockSpec((tm, d), lambda i: (i, col_ma // d)),
                  pl.BlockSpec((tm, d), lambda i: (i, col_mb // d)),
                  _resident((dr, d)), _resident((dr, d))],
        out_specs=pl.BlockSpec((tm, d), row),
        out_shape=jax.ShapeDtypeStruct((t, d), BF16),
        compiler_params=pltpu.CompilerParams(
            dimension_semantics=("parallel",),
            vmem_limit_bytes=_vmem_limit(est)),
        name="branch_merge",
    )(ya, o_f, o_b, z, nw, z, z, wa, wb)


def _out_proj_kernel(m_ref, wo_ref, y_ref, g_ref, b_ref, g2_ref, b2_ref, o_ref, xn_ref, *, alpha,
                     sub):
    for r0 in range(0, y_ref.shape[0], sub):
        rows = slice(r0, r0 + sub)
        y = (alpha * _layer_norm(y_ref[rows, :], g_ref[...], b_ref[...])
             + _dot(m_ref[rows, :], wo_ref[...]))
        o_ref[rows, :] = y
        xn_ref[rows, :] = _layer_norm(y, g2_ref[...], b2_ref[...]).astype(xn_ref.dtype)


def _out_proj_call(merged, wo, y, ln, ln_next, *, alpha):
    t, d = y.shape
    tm = _pick_tile(t, 512, BF16_ROWS)
    est = 2 * tm * d * 2 + d * d * 2 + 4 * tm * d * 4 + 2 * tm * d * 2 + 6 * tm * d * 4
    row = lambda i: (i, 0)
    return pl.pallas_call(
        functools.partial(_out_proj_kernel, alpha=alpha, sub=_pick_tile(tm, 256, BF16_ROWS)),
        grid=(t // tm,),
        in_specs=[pl.BlockSpec((tm, d), row),
                  _resident((d, d)),
                  pl.BlockSpec((tm, d), row),
                  _resident((1, d)), _resident((1, d)), _resident((1, d)), _resident((1, d))],
        out_specs=[pl.BlockSpec((tm, d), row), pl.BlockSpec((tm, d), row)],
        out_shape=[jax.ShapeDtypeStruct((t, d), F32), jax.ShapeDtypeStruct((t, d), BF16)],
        compiler_params=pltpu.CompilerParams(
            dimension_semantics=("parallel",),
            vmem_limit_bytes=_vmem_limit(est)),
        name="out_proj",
    )(merged, wo, y, ln[0], ln[1], ln_next[0], ln_next[1])


def _ffn_up_kernel(x_ref, xp_ref, xq_ref, wg_ref, wv_ref, cw_ref, cb_ref, o_ref, xs_ref,
                   *, tiles_per_seq):
    tm = x_ref.shape[0]
    halo = xp_ref.shape[0]
    i = pl.program_id(0)

    @pl.when(pl.program_id(1) == 0)
    def _():
        xs_ref[halo:halo + tm, :] = x_ref[...]
        has_prev = (i % tiles_per_seq) != 0
        has_next = (i % tiles_per_seq) != tiles_per_seq - 1
        xs_ref[0:halo, :] = jnp.where(has_prev, xp_ref[...], jnp.zeros_like(xp_ref))
        xs_ref[halo + tm:halo + tm + halo, :] = jnp.where(has_next, xq_ref[...], jnp.zeros_like(xq_ref))

    ug = _dot(xs_ref[...], wg_ref[...])
    uv = _dot(x_ref[...], wv_ref[...])
    n_rows = ug.shape[0]
    cw = cw_ref[...]
    conv = (cb_ref[...] + cw[0:1, :] * pltpu.roll(ug, 1, 0) + cw[1:2, :] * ug
            + cw[2:3, :] * pltpu.roll(ug, n_rows - 1, 0))
    conv = conv[halo:halo + tm, :]
    o_ref[...] = (_gelu_tanh(conv) * uv).astype(o_ref.dtype)


def _ffn_up_call(xn, w_up, cw, cb, seq_len):
    t, d = xn.shape
    d_ff = w_up.shape[1] // 2
    halo = BF16_ROWS
    tm = _pick_tile(seq_len, 1024, halo)
    tn = _pick_tile(d_ff, 768, LANES)
    tiles_per_seq = seq_len // tm
    hb = tm // halo
    n_hblocks = t // halo
    n_j = d_ff // tn
    est = (2 * tm * d * 2 + 4 * halo * d * 2 + (tm + 2 * halo) * d * 2 + 2 * 2 * d * tn * 2
           + 2 * tm * tn * 2 + 7 * (tm + 2 * halo) * tn * 4)
    return pl.pallas_call(
        functools.partial(_ffn_up_kernel, tiles_per_seq=tiles_per_seq),
        grid=(t // tm, n_j),
        in_specs=[pl.BlockSpec((tm, d), lambda i, j: (i, 0)),
                  pl.BlockSpec((halo, d), lambda i, j: (jnp.maximum(i * hb - 1, 0), 0)),
                  pl.BlockSpec((halo, d), lambda i, j: (jnp.minimum((i + 1) * hb, n_hblocks - 1), 0)),
                  pl.BlockSpec((d, tn), lambda i, j: (0, j)),
                  pl.BlockSpec((d, tn), lambda i, j: (0, n_j + j)),
                  pl.BlockSpec((3, tn), lambda i, j: (0, j)),
                  pl.BlockSpec((1, tn), lambda i, j: (0, j))],
        out_specs=pl.BlockSpec((tm, tn), lambda i, j: (i, j)),
        out_shape=jax.ShapeDtypeStruct((t, d_ff), BF16),
        scratch_shapes=[pltpu.VMEM((tm + 2 * halo, d), BF16)],
        compiler_params=pltpu.CompilerParams(
            dimension_semantics=("parallel", "arbitrary"),
            vmem_limit_bytes=_vmem_limit(est)),
        name="ffn_up",
    )(xn, xn, xn, w_up, w_up, cw, cb)


def _ffn_down_kernel(h_ref, w_ref, y_ref, g_ref, b_ref, g2_ref, b2_ref, *out_refs, alpha, last):
    y = alpha * _layer_norm(y_ref[...], g_ref[...], b_ref[...]) + _dot(h_ref[...], w_ref[...])
    xn = _layer_norm(y, g2_ref[...], b2_ref[...])
    if last:
        out_refs[0][...] = xn
    else:
        out_refs[0][...] = y
        out_refs[1][...] = xn.astype(out_refs[1].dtype)


def _ffn_down_call(h, w, y, ln, ln_next, *, alpha, last):
    t, d = y.shape
    d_ff = h.shape[1]
    tm = _pick_tile(t, 256, BF16_ROWS)
    est = d_ff * d * 2 + 2 * tm * d_ff * 2 + 4 * tm * d * 4 + 2 * tm * d * 2 + 6 * tm * d * 4
    row = lambda i: (i, 0)
    out_specs = [pl.BlockSpec((tm, d), row)]
    out_shape = [jax.ShapeDtypeStruct((t, d), F32)]
    if not last:
        out_specs.append(pl.BlockSpec((tm, d), row))
        out_shape.append(jax.ShapeDtypeStruct((t, d), BF16))
    return pl.pallas_call(
        functools.partial(_ffn_down_kernel, alpha=alpha, last=last),
        grid=(t // tm,),
        in_specs=[pl.BlockSpec((tm, d_ff), row),
                  _resident((d_ff, d)),
                  pl.BlockSpec((tm, d), row),
                  _resident((1, d)), _resident((1, d)), _resident((1, d)), _resident((1, d))],
        out_specs=out_specs,
        out_shape=out_shape,
        compiler_params=pltpu.CompilerParams(
            dimension_semantics=("parallel",),
            vmem_limit_bytes=_vmem_limit(est)),
        name="ffn_down",
    )(h, w, y, ln[0], ln[1], ln_next[0], ln_next[1])


def _prepare_layer(p, l, d_model):
    d_rnn = p["conv_rg_w"].shape[-1]
    dk_tot = p["gla_wg2"].shape[-1]
    rank = p["gla_wg2"].shape[-2]
    dv_tot = p["w_proj_b"].shape[1]
    block_w = p["rg_wa"].shape[3]
    dk, dv = dk_tot // N_HEADS, dv_tot // N_HEADS

    sizes = [d_rnn, d_rnn, dk_tot, dk_tot, dv_tot, dv_tot, rank, rank, d_model, d_model]
    offs = [0]
    for sz in sizes:
        offs.append(offs[-1] + sz)
    rx, rg, q, k, v, og, gf, gb, ma, mb = [slice(offs[i], offs[i + 1]) for i in range(10)]
    order = [rx, rg, ma, mb, q, k, v, og]
    w_in, b_in = p["w_in"][l], p["b_in"][l]
    w_main = jnp.concatenate([w_in[:, s] for s in order], axis=1).astype(BF16)
    b_main = jnp.concatenate([b_in[s] for s in order])[None, :]
    cols, c0 = {}, 0
    for name, s in zip(("rx", "rg", "ma", "mb", "q", "k", "v", "og"), order):
        cols[name] = c0
        c0 += s.stop - s.start
    w_gate = jnp.zeros((d_model, LANES), F32).at[:, :2 * rank].set(w_in[:, gf.start:gb.stop]).astype(BF16)
    b_gate = jnp.zeros((1, LANES), F32).at[0, :2 * rank].set(b_in[gf.start:gb.stop])

    c = RG_GROUP
    n_groups = d_rnn // c
    per = c // block_w

    def group_dense(w):
        wgrp = w.reshape(n_groups, per, block_w, block_w)
        eye = jnp.eye(per, dtype=w.dtype)
        return jnp.einsum("gpij,pq->gpiqj", wgrp, eye).reshape(n_groups, c, c)

    rg_w = jnp.concatenate([group_dense(p["rg_wa"][l, 0]), group_dense(p["rg_wx"][l, 0]),
                            group_dense(p["rg_wa"][l, 1]), group_dense(p["rg_wx"][l, 1])],
                           axis=2).astype(BF16)

    def grp(vec):
        return vec.reshape(n_groups, 1, c)

    rg_b = jnp.concatenate([grp(p["rg_ba"][l, 0]), grp(p["rg_bx"][l, 0]),
                            grp(p["rg_ba"][l, 1]), grp(p["rg_bx"][l, 1])], axis=2)
    rg_lam = jnp.concatenate([grp(p["rg_lam"][l, 0]), grp(p["rg_lam"][l, 1])], axis=2)
    conv_w = p["conv_rg_w"][l].reshape(-1, n_groups, c).transpose(1, 0, 2)
    conv_b = grp(p["conv_rg_b"][l])

    w_gate2 = jnp.zeros((LANES, 2 * dk_tot), F32)
    w_gate2 = w_gate2.at[:rank, :dk_tot].set(p["gla_wg2"][l, 0])
    w_gate2 = w_gate2.at[rank:2 * rank, dk_tot:].set(p["gla_wg2"][l, 1]).astype(BF16)
    b_gate2 = jnp.concatenate([p["gla_bg"][l, 0], p["gla_bg"][l, 1]])[None, :]

    return dict(
        cols=cols, d_rnn=d_rnn, dk=dk, dv=dv,
        w_main=w_main, b_main=b_main, w_gate=w_gate, b_gate=b_gate,
        rg_w=rg_w, rg_b=rg_b, rg_lam=rg_lam, conv_w=conv_w, conv_b=conv_b,
        w_gate2=w_gate2, b_gate2=b_gate2,
        norm_w=jnp.tile(p["gla_norm_w"][l], N_HEADS)[None, :],
        w_pa=p["w_proj_a"][l].astype(BF16), w_pb=p["w_proj_b"][l].astype(BF16),
        w_out=p["w_out"][l].astype(BF16),
        ln_mix=(p["ln_mix_g"][l][None, :], p["ln_mix_b"][l][None, :]),
        w_up=p["w_up"][l].astype(BF16), conv_ff_w=p["conv_ff_w"][l],
        conv_ff_b=p["conv_ff_b"][l][None, :], w_down=p["w_down"][l].astype(BF16),
        ln_ffn=(p["ln_ffn_g"][l][None, :], p["ln_ffn_b"][l][None, :]),
    )


def _trunk(x, ln_in, layers, alpha):
    b, s, d = x.shape
    y = x.reshape(b * s, d)
    ln = ln_in
    xn = _ln_cast_call(y, ln[0], ln[1])
    for li, lp in enumerate(layers):
        cols = lp["cols"]
        z, gz = _matmul_in_call(xn, lp["w_main"], lp["b_main"], lp["w_gate"], lp["b_gate"],
                                lp["w_gate2"], lp["b_gate2"])
        z3 = z.reshape(b, s, z.shape[1])
        gz3 = gz.reshape(b, s, gz.shape[1])
        ya = _rglru_call(z3, lp["conv_w"], lp["conv_b"], lp["rg_w"], lp["rg_b"], lp["rg_lam"],
                         lp["d_rnn"])
        gla = functools.partial(_gla_call, z3, gz3, col_q=cols["q"], col_k=cols["k"],
                                col_v=cols["v"], dk=lp["dk"], dv=lp["dv"])
        o_f = gla(reverse=False)
        o_b = gla(reverse=True)
        merged = _branch_merge_call(
            ya.reshape(b * s, -1), o_f.reshape(b * s, -1), o_b.reshape(b * s, -1), z, lp["norm_w"],
            lp["w_pa"], lp["w_pb"], col_og=cols["og"], col_ma=cols["ma"], col_mb=cols["mb"],
            dv=lp["dv"])
        y_mix, xn_mix = _out_proj_call(merged, lp["w_out"], y, ln, lp["ln_mix"], alpha=alpha)
        hdn = _ffn_up_call(xn_mix, lp["w_up"], lp["conv_ff_w"], lp["conv_ff_b"], s)
        last = li == len(layers) - 1
        outs = _ffn_down_call(hdn, lp["w_down"], y_mix, lp["ln_mix"], lp["ln_ffn"], alpha=alpha,
                              last=last)
        if last:
            return outs[0].reshape(b, s, d)
        y, xn = outs
        ln = lp["ln_ffn"]


def kernel(x_prompt, x_sample, ln_in_g, ln_in_b, w_in, b_in, conv_rg_w, conv_rg_b, rg_wa, rg_ba, rg_wx, rg_bx, rg_lam, gla_wg2, gla_bg, gla_norm_w, w_proj_a, w_proj_b, w_out, ln_mix_g, ln_mix_b, w_up, conv_ff_w, conv_ff_b, w_down, ln_ffn_g, ln_ffn_b):
    p = dict(w_in=w_in, b_in=b_in, conv_rg_w=conv_rg_w, conv_rg_b=conv_rg_b, rg_wa=rg_wa,
             rg_ba=rg_ba, rg_wx=rg_wx, rg_bx=rg_bx, rg_lam=rg_lam, gla_wg2=gla_wg2, gla_bg=gla_bg,
             gla_norm_w=gla_norm_w, w_proj_a=w_proj_a, w_proj_b=w_proj_b, w_out=w_out,
             ln_mix_g=ln_mix_g, ln_mix_b=ln_mix_b, w_up=w_up, conv_ff_w=conv_ff_w,
             conv_ff_b=conv_ff_b, w_down=w_down, ln_ffn_g=ln_ffn_g, ln_ffn_b=ln_ffn_b)
    depth = w_in.shape[0]
    d_model = w_in.shape[1]
    alpha = (2.0 * depth) ** 0.25
    layers = [_prepare_layer(p, l, d_model) for l in range(depth)]
    ln_in = (ln_in_g[None, :], ln_in_b[None, :])
    return (_trunk(x_prompt, ln_in, layers, alpha), _trunk(x_sample, ln_in, layers, alpha))
```

```python
import functools
import math

import jax
import jax.numpy as jnp
from jax import lax
from jax.experimental import pallas as pl
from jax.experimental.pallas import tpu as pltpu

F32 = jnp.float32
BF16 = jnp.bfloat16

V7X_VMEM_BYTES = 64 * 1024 * 1024
LANES = 128
SUBLANES = 8
BF16_ROWS = 16

N_HEADS = 4
CHUNK = 64
RG_C = 8.0
GATE_NORM = 16.0
EPS = 1e-5
RG_GROUP = 128
SCAN_BLOCK = SUBLANES * SUBLANES
GLA_SUB = 256


def _vmem_limit(nbytes):
    return int(min(max(nbytes, 16 * 1024 * 1024), V7X_VMEM_BYTES - 6 * 1024 * 1024))


def _dot(a, b):
    return jnp.dot(a, b, preferred_element_type=F32)


def _layer_norm(x, g, b):
    mu = jnp.mean(x, axis=-1, keepdims=True)
    xc = x - mu
    var = jnp.mean(xc * xc, axis=-1, keepdims=True)
    return xc * lax.rsqrt(var + EPS) * g + b


def _softplus(x):
    return jnp.maximum(x, 0.0) + jnp.log1p(jnp.exp(-jnp.abs(x)))


def _gelu_tanh(x):
    c = math.sqrt(2.0 / math.pi)
    return 0.5 * x * (1.0 + jnp.tanh(c * (x + 0.044715 * (x * x * x))))


def _pick_tile(n, target, multiple):
    t = min(n, target)
    while t > multiple and (n % t or t % multiple):
        t -= multiple
    assert n % t == 0 and t % multiple == 0, (n, target, multiple)
    return t


def _resident(shape):
    return pl.BlockSpec(shape, lambda *_: (0,) * len(shape), pipeline_mode=pl.Buffered(1))


def _ln_cast_kernel(y_ref, g_ref, b_ref, o_ref):
    o_ref[...] = _layer_norm(y_ref[...], g_ref[...], b_ref[...]).astype(o_ref.dtype)


def _ln_cast_call(y, g, b):
    t, d = y.shape
    tm = _pick_tile(t, 512, BF16_ROWS)
    return pl.pallas_call(
        _ln_cast_kernel,
        grid=(t // tm,),
        in_specs=[pl.BlockSpec((tm, d), lambda i: (i, 0)), _resident((1, d)), _resident((1, d))],
        out_specs=pl.BlockSpec((tm, d), lambda i: (i, 0)),
        out_shape=jax.ShapeDtypeStruct((t, d), BF16),
        compiler_params=pltpu.CompilerParams(
            dimension_semantics=("parallel",),
            vmem_limit_bytes=_vmem_limit(6 * tm * d * 4)),
        name="ln_cast",
    )(y, g, b)


def _matmul_in_kernel(x_ref, w_ref, bias_ref, wg_ref, bg_ref, wg2_ref, bg2_ref, z_ref, gz_ref,
                      *, gate_rows):
    n_slices = x_ref.shape[0] // gate_rows
    r0 = pl.multiple_of(jnp.minimum(pl.program_id(1), n_slices - 1) * gate_rows, gate_rows)
    zg = _dot(x_ref[pl.ds(r0, gate_rows), :], wg_ref[...]) + bg_ref[...]
    z_ref[...] = _dot(x_ref[...], w_ref[...]) + bias_ref[...]
    gz = _dot(zg.astype(BF16), wg2_ref[...]) + bg2_ref[...]
    gz_ref[pl.ds(r0, gate_rows), :] = (
        (jnp.minimum(gz, 0.0) - jnp.log(1.0 + jnp.exp(-jnp.abs(gz)))) * (1.0 / GATE_NORM))


def _matmul_in_call(xn, w, bias, wg, bg, wg2, bg2):
    t, d = xn.shape
    n = w.shape[1]
    ng = wg2.shape[1]
    tm = _pick_tile(t, 1024, BF16_ROWS)
    tn = _pick_tile(n, 1536, 2 * LANES)
    gate_rows = _pick_tile(tm, tm, BF16_ROWS)
    while tm % (gate_rows // 2) == 0 and (gate_rows // 2) % BF16_ROWS == 0 \
            and (tm // (gate_rows // 2)) <= n // tn:
        gate_rows //= 2
    est = (2 * tm * d * 2 + 2 * d * tn * 2 + 3 * tm * tn * 4 + d * LANES * 2 + tm * LANES * 4
           + 3 * tm * ng * 4)
    return pl.pallas_call(
        functools.partial(_matmul_in_kernel, gate_rows=gate_rows),
        grid=(t // tm, n // tn),
        in_specs=[pl.BlockSpec((tm, d), lambda i, j: (i, 0)),
                  pl.BlockSpec((d, tn), lambda i, j: (0, j)),
                  pl.BlockSpec((1, tn), lambda i, j: (0, j)),
                  _resident((d, LANES)),
                  _resident((1, LANES)),
                  _resident((LANES, ng)),
                  _resident((1, ng))],
        out_specs=[pl.BlockSpec((tm, tn), lambda i, j: (i, j)),
                   pl.BlockSpec((tm, ng), lambda i, j: (i, 0))],
        out_shape=[jax.ShapeDtypeStruct((t, n), F32),
                   jax.ShapeDtypeStruct((t, ng), F32)],
        compiler_params=pltpu.CompilerParams(
            dimension_semantics=("parallel", "arbitrary"),
            vmem_limit_bytes=_vmem_limit(est)),
        name="matmul_in",
    )(xn, w, bias, wg, bg, wg2, bg2)


def _block_transpose(x):
    rows, c = x.shape
    blocks = rows // SCAN_BLOCK
    return jnp.swapaxes(x.reshape(blocks, SUBLANES, SUBLANES, c), 1, 2).reshape(rows, c)


def _block_scan(a, u, reverse):
    rows, c = a.shape
    blocks = rows // SCAN_BLOCK
    at = a.reshape(blocks, SUBLANES, SUBLANES, c)
    ut = u.reshape(blocks, SUBLANES, SUBLANES, c)
    order = list(range(SUBLANES - 1, -1, -1) if reverse else range(SUBLANES))
    p, h = {}, {}
    prev = None
    for j in order:
        if prev is None:
            p[j], h[j] = at[:, j], ut[:, j]
        else:
            p[j], h[j] = at[:, j] * p[prev], at[:, j] * h[prev] + ut[:, j]
        prev = j
    q, e = p[prev], h[prev]
    row = lax.broadcasted_iota(jnp.int32, q.shape, 1)
    for d in (1, 2, 4):
        shift = SUBLANES - d if reverse else d
        valid = (row < SUBLANES - d) if reverse else (row >= d)
        q_s = pltpu.roll(q, shift, 1)
        e_s = pltpu.roll(e, shift, 1)
        e = jnp.where(valid, q * e_s + e, e)
        q = jnp.where(valid, q * q_s, q)
    shift = SUBLANES - 1 if reverse else 1
    first = (row == SUBLANES - 1) if reverse else (row == 0)
    q_in = jnp.where(first, 1.0, pltpu.roll(q, shift, 1))
    e_in = jnp.where(first, 0.0, pltpu.roll(e, shift, 1))
    a_out = jnp.stack([p[j] * q_in for j in range(SUBLANES)], axis=1)
    u_out = jnp.stack([h[j] + p[j] * e_in for j in range(SUBLANES)], axis=1)
    return a_out.reshape(rows, c), u_out.reshape(rows, c)


def _rglru_kernel(zrx_ref, zrg_ref, cw_ref, cb_ref, wg_ref, bg_ref, lam_ref, o_ref,
                  xpad_ref, af_ref, uf_ref, ab_ref, ub_ref, *, tile):
    s, c = zrx_ref.shape
    n_tiles = s // tile
    n_groups = s // SCAN_BLOCK
    pad = SUBLANES

    xpad_ref[0:pad, :] = jnp.zeros((pad, c), F32)
    xpad_ref[pad + s:pad + s + pad, :] = jnp.zeros((pad, c), F32)

    def copy_body(i, _):
        t0 = pl.multiple_of(i * tile, tile)
        xpad_ref[pl.ds(pad + t0, tile), :] = zrx_ref[pl.ds(t0, tile), :]
        return 0

    lax.fori_loop(0, n_tiles, copy_body, 0)

    cw = cw_ref[...]
    cb = cb_ref[...]
    bg = bg_ref[...]
    decay = -RG_C * _softplus(-lam_ref[...])

    def local_body(i, _):
        t0 = pl.multiple_of(i * tile, tile)
        xa = cb
        for j in range(4):
            xa = xa + cw[j:j + 1, :] * xpad_ref[pl.ds(pad + t0 - 2 + j, tile), :]
        xa = _block_transpose(xa)
        gz = _dot(xa.astype(BF16), wg_ref[...]) + bg
        for direction, (a_ref, u_ref) in enumerate(((af_ref, uf_ref), (ab_ref, ub_ref))):
            lo = 2 * c * direction
            r = jax.nn.sigmoid(gz[:, lo:lo + c])
            ig = jax.nn.sigmoid(gz[:, lo + c:lo + 2 * c])
            log_a = r * decay[:, c * direction:c * (direction + 1)]
            a = jnp.exp(log_a)
            th = jnp.tanh(log_a)
            one_minus_a2 = (-2.0 * th) / (1.0 - th)
            root = jnp.where(one_minus_a2 > 0.0, one_minus_a2 * lax.rsqrt(one_minus_a2), 0.0)
            u = root * (ig * xa)
            a_run, u_run = _block_scan(a, u, reverse=bool(direction))
            a_ref[pl.ds(t0, tile), :] = a_run
            u_ref[pl.ds(t0, tile), :] = u_run
        return 0

    lax.fori_loop(0, n_tiles, local_body, 0)

    def chain_body(g, carry):
        h_f, h_b = carry
        rf = pl.multiple_of(g * SCAN_BLOCK, SCAN_BLOCK)
        rb = pl.multiple_of((n_groups - 1 - g) * SCAN_BLOCK, SCAN_BLOCK)
        hf = af_ref[pl.ds(rf, SCAN_BLOCK), :] * h_f + uf_ref[pl.ds(rf, SCAN_BLOCK), :]
        uf_ref[pl.ds(rf, SCAN_BLOCK), :] = hf
        hb = ab_ref[pl.ds(rb, SCAN_BLOCK), :] * h_b + ub_ref[pl.ds(rb, SCAN_BLOCK), :]
        ub_ref[pl.ds(rb, SCAN_BLOCK), :] = hb
        return hf[SCAN_BLOCK - 1:SCAN_BLOCK], hb[0:1]

    zero = jnp.zeros((1, c), F32)
    lax.fori_loop(0, n_groups, chain_body, (zero, zero), unroll=4)

    def out_body(i, _):
        t0 = pl.multiple_of(i * tile, tile)
        h = _block_transpose(uf_ref[pl.ds(t0, tile), :] + ub_ref[pl.ds(t0, tile), :])
        o_ref[pl.ds(t0, tile), :] = (_gelu_tanh(zrg_ref[pl.ds(t0, tile), :]) * h).astype(o_ref.dtype)
        return 0

    lax.fori_loop(0, n_tiles, out_body, 0)


def _rglru_call(z3, cw, cb, wg, bg, lam, d_rnn):
    b, s, _ = z3.shape
    c = RG_GROUP
    n_groups = d_rnn // c
    tile = _pick_tile(s, 512, SCAN_BLOCK)
    est = (2 * 2 * s * c * 4 + 2 * s * c * 2 + 5 * s * c * 4 + 2 * SUBLANES * c * 4
           + 40 * tile * c * 4 + 4 * c * 4 * c * 2)
    return pl.pallas_call(
        functools.partial(_rglru_kernel, tile=tile),
        grid=(b, n_groups),
        in_specs=[pl.BlockSpec((None, s, c), lambda i, j: (i, 0, j)),
                  pl.BlockSpec((None, s, c), lambda i, j: (i, 0, n_groups + j)),
                  pl.BlockSpec((None, 4, c), lambda i, j: (j, 0, 0)),
                  pl.BlockSpec((None, 1, c), lambda i, j: (j, 0, 0)),
                  pl.BlockSpec((None, c, 4 * c), lambda i, j: (j, 0, 0)),
                  pl.BlockSpec((None, 1, 4 * c), lambda i, j: (j, 0, 0)),
                  pl.BlockSpec((None, 1, 2 * c), lambda i, j: (j, 0, 0))],
        out_specs=pl.BlockSpec((None, s, c), lambda i, j: (i, 0, j)),
        out_shape=jax.ShapeDtypeStruct((b, s, d_rnn), BF16),
        scratch_shapes=[pltpu.VMEM((s + 2 * SUBLANES, c), F32)] + [pltpu.VMEM((s, c), F32)] * 4,
        compiler_params=pltpu.CompilerParams(
            dimension_semantics=("parallel", "parallel"),
            vmem_limit_bytes=_vmem_limit(est)),
        name="rglru",
    )(z3, z3, cw, cb, wg, bg, lam)


def _chunk_cumsum(g, reverse):
    rows, c = g.shape
    groups = rows // SUBLANES
    per_chunk = CHUNK // SUBLANES
    x = g.reshape(groups, SUBLANES, c)
    row = lax.broadcasted_iota(jnp.int32, (groups, SUBLANES, c), 1)
    for d in (1, 2, 4):
        shift = SUBLANES - d if reverse else d
        valid = (row < SUBLANES - d) if reverse else (row >= d)
        x = x + jnp.where(valid, pltpu.roll(x, shift, 1), 0.0)
    outs = [None] * groups
    for ch in range(groups // per_chunk):
        carry = None
        order = range(per_chunk - 1, -1, -1) if reverse else range(per_chunk)
        for gi in order:
            idx = ch * per_chunk + gi
            xg = x[idx] if carry is None else x[idx] + carry
            outs[idx] = xg
            carry = xg[0:1] if reverse else xg[SUBLANES - 1:SUBLANES]
    return jnp.concatenate(outs, axis=0)


def _gla_kernel(q_ref, k_ref, v_ref, g_ref, o_ref, st_ref, *, reverse, scale):
    rows, dk = q_ref.shape
    pair = 2 * CHUNK
    n_chunks = rows // CHUNK
    assert dk == LANES and rows % pair == 0 and n_chunks <= LANES

    @pl.when(pl.program_id(2) == 0)
    def _():
        st_ref[...] = jnp.zeros_like(st_ref)

    ti = lax.broadcasted_iota(jnp.int32, (CHUNK, pair), 0)
    si = lax.broadcasted_iota(jnp.int32, (CHUNK, pair), 1)
    h_a = 1 if reverse else 0
    h_b = 1 - h_a
    own = {}
    for half in range(2):
        local = si - half * CHUNK
        causal = (local >= ti) if reverse else (local <= ti)
        own[half] = (local >= 0) & (local < CHUNK) & causal
    mask_a = own[h_a]
    other_half = (si >= h_a * CHUNK) & (si < (h_a + 1) * CHUNK)
    mask_b = own[h_b] | other_half

    def by_half(x_a, x_b, axis=0):
        return jnp.concatenate([x_a, x_b] if h_a == 0 else [x_b, x_a], axis=axis)

    nt = (((1,), (1,)), ((), ()))
    tn = (((0,), (0,)), ((), ()))

    sub = min(GLA_SUB, rows)
    n_sub = rows // sub
    per_sub = sub // CHUNK
    n_pairs = per_sub // 2
    sub_order = list(range(n_sub - 1, -1, -1) if reverse else range(n_sub))
    pair_order = list(range(n_pairs - 1, -1, -1) if reverse else range(n_pairs))

    def stage_a(sb):
        r0 = sb * sub
        gc_s = _chunk_cumsum(g_ref[r0:r0 + sub, :], reverse)
        g_last = [gc_s[c * CHUNK:c * CHUNK + 1] if reverse
                  else gc_s[(c + 1) * CHUNK - 1:(c + 1) * CHUNK] for c in range(per_sub)]
        g_last_rows = jnp.concatenate([jnp.broadcast_to(r, (CHUNK, dk)) for r in g_last], axis=0)
        g_pair = [g_last[2 * p] + g_last[2 * p + 1] for p in range(n_pairs)]
        dec_t = jnp.exp(jnp.concatenate(g_pair + [jnp.zeros((LANES - n_pairs, dk), F32)], axis=0).T)
        k = k_ref[r0:r0 + sub, :]
        qd = q_ref[r0:r0 + sub, :] * scale * jnp.exp(gc_s)
        kd = (k * jnp.exp(-gc_s)).astype(BF16)
        k_end = k * jnp.exp(g_last_rows - gc_s)
        out = []
        for p in pair_order:
            ra = slice((2 * p + h_a) * CHUNK, (2 * p + h_a + 1) * CHUNK)
            rb = slice((2 * p + h_b) * CHUNK, (2 * p + h_b + 1) * CHUNK)
            dec_a = jnp.exp(g_last[2 * p + h_a])
            dec_b = jnp.exp(g_last[2 * p + h_b])
            q_a = qd[ra].astype(BF16)
            q_b = qd[rb].astype(BF16)
            v_pair = v_ref[r0 + p * pair:r0 + (p + 1) * pair, :].astype(BF16)
            att_a = lax.dot_general(q_a, kd[p * pair:(p + 1) * pair], nt, preferred_element_type=F32)
            att_a = jnp.where(mask_a, att_a, 0.0).astype(BF16)
            keys_b = by_half(k_end[ra].astype(BF16), kd[rb])
            att_b = lax.dot_general(q_b, keys_b, nt, preferred_element_type=F32)
            att_b = jnp.where(mask_b, att_b, 0.0).astype(BF16)
            keys_s = by_half((k_end[ra] * dec_b).astype(BF16), k_end[rb].astype(BF16))
            kv = lax.dot_general(keys_s, v_pair, tn, preferred_element_type=F32)
            lhs = by_half(jnp.concatenate([q_a, att_a], axis=1),
                          jnp.concatenate([(qd[rb] * dec_a).astype(BF16), att_b], axis=1))
            out.append(dict(row0=r0 + p * pair, lhs=lhs, v_pair=v_pair, kv=kv,
                            dec=dec_t[:, p:p + 1]))
        return out

    def stage_bc(pairs, st):
        st_in = []
        for a in pairs:
            st_in.append(st.astype(BF16))
            st = a["dec"] * st + a["kv"]
        for a, s_in in zip(pairs, st_in):
            o_ref[a["row0"]:a["row0"] + pair, :] = _dot(
                a["lhs"], jnp.concatenate([s_in, a["v_pair"]], axis=0))
        return st

    st = st_ref[...]
    ahead = stage_a(sub_order[0])
    for i in range(n_sub):
        cur = ahead
        if i + 1 < n_sub:
            ahead = stage_a(sub_order[i + 1])
        st = stage_bc(cur, st)
    st_ref[...] = st


def _gla_call(z3, gz3, *, reverse, col_q, col_k, col_v, dk, dv):
    b, s, _ = z3.shape
    rows = _pick_tile(s, 2048, 2 * CHUNK)
    nb = s // rows
    col_g = N_HEADS * dk if reverse else 0

    def blk(j):
        return nb - 1 - j if reverse else j

    est = 2 * rows * (3 * dk + 2 * dv) * 4 + 40 * rows * dk * 4 + 24 * dv * dk * 4
    return pl.pallas_call(
        functools.partial(_gla_kernel, reverse=reverse, scale=dk ** -0.5),
        grid=(b, N_HEADS, nb),
        in_specs=[pl.BlockSpec((None, rows, dk), lambda i, h, j: (i, blk(j), col_q // dk + h)),
                  pl.BlockSpec((None, rows, dk), lambda i, h, j: (i, blk(j), col_k // dk + h)),
                  pl.BlockSpec((None, rows, dv), lambda i, h, j: (i, blk(j), col_v // dv + h)),
                  pl.BlockSpec((None, rows, dk), lambda i, h, j: (i, blk(j), col_g // dk + h))],
        out_specs=pl.BlockSpec((None, rows, dv), lambda i, h, j: (i, blk(j), h)),
        out_shape=jax.ShapeDtypeStruct((b, s, N_HEADS * dv), F32),
        scratch_shapes=[pltpu.VMEM((dk, dv), F32)],
        compiler_params=pltpu.CompilerParams(
            dimension_semantics=("parallel", "parallel", "arbitrary"),
            vmem_limit_bytes=_vmem_limit(est)),
        name="gla_bwd" if reverse else "gla_fwd",
    )(z3, z3, z3, gz3)


def _mix_merge_kernel(ya_ref, of_ref, ob_ref, og_ref, nw_ref, ma_ref, mb_ref, y_ref, g_ref, b_ref,
                      wa_ref, wb_ref, wo_ref, g2_ref, b2_ref, o_ref, xn_ref, *, alpha, dv):
    y_a = _dot(ya_ref[...], wa_ref[...])
    o = of_ref[...] + ob_ref[...]
    parts = []
    for h in range(o.shape[1] // dv):
        oh = o[:, h * dv:(h + 1) * dv]
        parts.append(oh * lax.rsqrt(jnp.mean(oh * oh, axis=-1, keepdims=True) + EPS))
    og = og_ref[...]
    yb = (jnp.concatenate(parts, axis=1) * nw_ref[...] * (og * jax.nn.sigmoid(og))).astype(BF16)
    y_b = _dot(yb, wb_ref[...])
    merged = jax.nn.sigmoid(ma_ref[...]) * y_a + jax.nn.sigmoid(mb_ref[...]) * y_b
    mix = _dot(merged.astype(BF16), wo_ref[...])
    y = alpha * _layer_norm(y_ref[...], g_ref[...], b_ref[...]) + mix
    o_ref[...] = y
    xn_ref[...] = _layer_norm(y, g2_ref[...], b2_ref[...]).astype(xn_ref.dtype)


def _mix_merge_call(ya, o_f, o_b, z, nw, y, ln, wa, wb, wo, ln_next, *, col_og, col_ma, col_mb,
                    alpha, dv):
    t, d = y.shape
    dr = ya.shape[1]
    tm = _pick_tile(t, 256, BF16_ROWS)
    est = (2 * tm * dr * 2 + 2 * 3 * tm * dr * 4 + 2 * 4 * tm * d * 4 + 2 * tm * d * 2
           + (2 * dr * d + d * d) * 2 + 8 * tm * d * 4)
    row = lambda i: (i, 0)
    return pl.pallas_call(
        functools.partial(_mix_merge_kernel, alpha=alpha, dv=dv),
        grid=(t // tm,),
        in_specs=[pl.BlockSpec((tm, dr), row),
                  pl.BlockSpec((tm, dr), row),
                  pl.BlockSpec((tm, dr), row),
                  pl.BlockSpec((tm, dr), lambda i: (i, col_og // dr)),
                  _resident((1, dr)),
                  pl.BlockSpec((tm, d), lambda i: (i, col_ma // d)),
                  pl.BlockSpec((tm, d), lambda i: (i, col_mb // d)),
                  pl.BlockSpec((tm, d), row),
                  _resident((1, d)), _resident((1, d)),
                  _resident((dr, d)), _resident((dr, d)), _resident((d, d)),
                  _resident((1, d)), _resident((1, d))],
        out_specs=[pl.BlockSpec((tm, d), row), pl.BlockSpec((tm, d), row)],
        out_shape=[jax.ShapeDtypeStruct((t, d), F32), jax.ShapeDtypeStruct((t, d), BF16)],
        compiler_params=pltpu.CompilerParams(
            dimension_semantics=("parallel",),
            vmem_limit_bytes=_vmem_limit(est)),
        name="mix_merge",
    )(ya, o_f, o_b, z, nw, z, z, y, ln[0], ln[1], wa, wb, wo, ln_next[0], ln_next[1])


def _ffn_up_kernel(x_ref, xp_ref, xq_ref, wg_ref, wv_ref, cw_ref, cb_ref, o_ref, xs_ref,
                   *, tiles_per_seq):
    tm = x_ref.shape[0]
    halo = xp_ref.shape[0]
    i = pl.program_id(0)

    @pl.when(pl.program_id(1) == 0)
    def _():
        xs_ref[halo:halo + tm, :] = x_ref[...]
        has_prev = (i % tiles_per_seq) != 0
        has_next = (i % tiles_per_seq) != tiles_per_seq - 1
        xs_ref[0:halo, :] = jnp.where(has_prev, xp_ref[...], jnp.zeros_like(xp_ref))
        xs_ref[halo + tm:halo + tm + halo, :] = jnp.where(has_next, xq_ref[...], jnp.zeros_like(xq_ref))

    ug = _dot(xs_ref[...], wg_ref[...])
    uv = _dot(x_ref[...], wv_ref[...])
    n_rows = ug.shape[0]
    cw = cw_ref[...]
    conv = (cb_ref[...] + cw[0:1, :] * pltpu.roll(ug, 1, 0) + cw[1:2, :] * ug
            + cw[2:3, :] * pltpu.roll(ug, n_rows - 1, 0))
    conv = conv[halo:halo + tm, :]
    o_ref[...] = (_gelu_tanh(conv) * uv).astype(o_ref.dtype)


def _ffn_up_call(xn, w_up, cw, cb, seq_len):
    t, d = xn.shape
    d_ff = w_up.shape[1] // 2
    halo = BF16_ROWS
    tm = _pick_tile(seq_len, 1024, halo)
    tn = _pick_tile(d_ff, 768, LANES)
    tiles_per_seq = seq_len // tm
    hb = tm // halo
    n_hblocks = t // halo
    n_j = d_ff // tn
    est = (2 * tm * d * 2 + 4 * halo * d * 2 + (tm + 2 * halo) * d * 2 + 2 * 2 * d * tn * 2
           + 2 * tm * tn * 2 + 7 * (tm + 2 * halo) * tn * 4)
    return pl.pallas_call(
        functools.partial(_ffn_up_kernel, tiles_per_seq=tiles_per_seq),
        grid=(t // tm, n_j),
        in_specs=[pl.BlockSpec((tm, d), lambda i, j: (i, 0)),
                  pl.BlockSpec((halo, d), lambda i, j: (jnp.maximum(i * hb - 1, 0), 0)),
                  pl.BlockSpec((halo, d), lambda i, j: (jnp.minimum((i + 1) * hb, n_hblocks - 1), 0)),
                  pl.BlockSpec((d, tn), lambda i, j: (0, j)),
                  pl.BlockSpec((d, tn), lambda i, j: (0, n_j + j)),
                  pl.BlockSpec((3, tn), lambda i, j: (0, j)),
                  pl.BlockSpec((1, tn), lambda i, j: (0, j))],
        out_specs=pl.BlockSpec((tm, tn), lambda i, j: (i, j)),
        out_shape=jax.ShapeDtypeStruct((t, d_ff), BF16),
        scratch_shapes=[pltpu.VMEM((tm + 2 * halo, d), BF16)],
        compiler_params=pltpu.CompilerParams(
            dimension_semantics=("parallel", "arbitrary"),
            vmem_limit_bytes=_vmem_limit(est)),
        name="ffn_up",
    )(xn, xn, xn, w_up, w_up, cw, cb)


def _ffn_down_kernel(h_ref, w_ref, y_ref, g_ref, b_ref, g2_ref, b2_ref, *out_refs, alpha, last):
    y = alpha * _layer_norm(y_ref[...], g_ref[...], b_ref[...]) + _dot(h_ref[...], w_ref[...])
    xn = _layer_norm(y, g2_ref[...], b2_ref[...])
    if last:
        out_refs[0][...] = xn
    else:
        out_refs[0][...] = y
        out_refs[1][...] = xn.astype(out_refs[1].dtype)


def _ffn_down_call(h, w, y, ln, ln_next, *, alpha, last):
    t, d = y.shape
    d_ff = h.shape[1]
    tm = _pick_tile(t, 256, BF16_ROWS)
    est = d_ff * d * 2 + 2 * tm * d_ff * 2 + 4 * tm * d * 4 + 2 * tm * d * 2 + 6 * tm * d * 4
    row = lambda i: (i, 0)
    out_specs = [pl.BlockSpec((tm, d), row)]
    out_shape = [jax.ShapeDtypeStruct((t, d), F32)]
    if not last:
        out_specs.append(pl.BlockSpec((tm, d), row))
        out_shape.append(jax.ShapeDtypeStruct((t, d), BF16))
    return pl.pallas_call(
        functools.partial(_ffn_down_kernel, alpha=alpha, last=last),
        grid=(t // tm,),
        in_specs=[pl.BlockSpec((tm, d_ff), row),
                  _resident((d_ff, d)),
                  pl.BlockSpec((tm, d), row),
                  _resident((1, d)), _resident((1, d)), _resident((1, d)), _resident((1, d))],
        out_specs=out_specs,
        out_shape=out_shape,
        compiler_params=pltpu.CompilerParams(
            dimension_semantics=("parallel",),
            vmem_limit_bytes=_vmem_limit(est)),
        name="ffn_down",
    )(h, w, y, ln[0], ln[1], ln_next[0], ln_next[1])


def _prepare_layer(p, l, d_model):
    d_rnn = p["conv_rg_w"].shape[-1]
    dk_tot = p["gla_wg2"].shape[-1]
    rank = p["gla_wg2"].shape[-2]
    dv_tot = p["w_proj_b"].shape[1]
    block_w = p["rg_wa"].shape[3]
    dk, dv = dk_tot // N_HEADS, dv_tot // N_HEADS

    sizes = [d_rnn, d_rnn, dk_tot, dk_tot, dv_tot, dv_tot, rank, rank, d_model, d_model]
    offs = [0]
    for sz in sizes:
        offs.append(offs[-1] + sz)
    rx, rg, q, k, v, og, gf, gb, ma, mb = [slice(offs[i], offs[i + 1]) for i in range(10)]
    order = [rx, rg, ma, mb, q, k, v, og]
    w_in, b_in = p["w_in"][l], p["b_in"][l]
    w_main = jnp.concatenate([w_in[:, s] for s in order], axis=1).astype(BF16)
    b_main = jnp.concatenate([b_in[s] for s in order])[None, :]
    cols, c0 = {}, 0
    for name, s in zip(("rx", "rg", "ma", "mb", "q", "k", "v", "og"), order):
        cols[name] = c0
        c0 += s.stop - s.start
    w_gate = jnp.zeros((d_model, LANES), F32).at[:, :2 * rank].set(w_in[:, gf.start:gb.stop]).astype(BF16)
    b_gate = jnp.zeros((1, LANES), F32).at[0, :2 * rank].set(b_in[gf.start:gb.stop])

    c = RG_GROUP
    n_groups = d_rnn // c
    per = c // block_w

    def group_dense(w):
        wgrp = w.reshape(n_groups, per, block_w, block_w)
        eye = jnp.eye(per, dtype=w.dtype)
        return jnp.einsum("gpij,pq->gpiqj", wgrp, eye).reshape(n_groups, c, c)

    rg_w = jnp.concatenate([group_dense(p["rg_wa"][l, 0]), group_dense(p["rg_wx"][l, 0]),
                            group_dense(p["rg_wa"][l, 1]), group_dense(p["rg_wx"][l, 1])],
                           axis=2).astype(BF16)

    def grp(vec):
        return vec.reshape(n_groups, 1, c)

    rg_b = jnp.concatenate([grp(p["rg_ba"][l, 0]), grp(p["rg_bx"][l, 0]),
                            grp(p["rg_ba"][l, 1]), grp(p["rg_bx"][l, 1])], axis=2)
    rg_lam = jnp.concatenate([grp(p["rg_lam"][l, 0]), grp(p["rg_lam"][l, 1])], axis=2)
    conv_w = p["conv_rg_w"][l].reshape(-1, n_groups, c).transpose(1, 0, 2)
    conv_b = grp(p["conv_rg_b"][l])

    w_gate2 = jnp.zeros((LANES, 2 * dk_tot), F32)
    w_gate2 = w_gate2.at[:rank, :dk_tot].set(p["gla_wg2"][l, 0])
    w_gate2 = w_gate2.at[rank:2 * rank, dk_tot:].set(p["gla_wg2"][l, 1]).astype(BF16)
    b_gate2 = jnp.concatenate([p["gla_bg"][l, 0], p["gla_bg"][l, 1]])[None, :]

    return dict(
        cols=cols, d_rnn=d_rnn, dk=dk, dv=dv,
        w_main=w_main, b_main=b_main, w_gate=w_gate, b_gate=b_gate,
        rg_w=rg_w, rg_b=rg_b, rg_lam=rg_lam, conv_w=conv_w, conv_b=conv_b,
        w_gate2=w_gate2, b_gate2=b_gate2,
        norm_w=jnp.tile(p["gla_norm_w"][l], N_HEADS)[None, :],
        w_pa=p["w_proj_a"][l].astype(BF16), w_pb=p["w_proj_b"][l].astype(BF16),
        w_out=p["w_out"][l].astype(BF16),
        ln_mix=(p["ln_mix_g"][l][None, :], p["ln_mix_b"][l][None, :]),
        w_up=p["w_up"][l].astype(BF16), conv_ff_w=p["conv_ff_w"][l],
        conv_ff_b=p["conv_ff_b"][l][None, :], w_down=p["w_down"][l].astype(BF16),
        ln_ffn=(p["ln_ffn_g"][l][None, :], p["ln_ffn_b"][l][None, :]),
    )


def _trunk(x, ln_in, layers, alpha):
    b, s, d = x.shape
    y = x.reshape(b * s, d)
    ln = ln_in
    xn = _ln_cast_call(y, ln[0], ln[1])
    for li, lp in enumerate(layers):
        cols = lp["cols"]
        z, gz = _matmul_in_call(xn, lp["w_main"], lp["b_main"], lp["w_gate"], lp["b_gate"],
                                lp["w_gate2"], lp["b_gate2"])
        z3 = z.reshape(b, s, z.shape[1])
        gz3 = gz.reshape(b, s, gz.shape[1])
        ya = _rglru_call(z3, lp["conv_w"], lp["conv_b"], lp["rg_w"], lp["rg_b"], lp["rg_lam"],
                         lp["d_rnn"])
        gla = functools.partial(_gla_call, z3, gz3, col_q=cols["q"], col_k=cols["k"],
                                col_v=cols["v"], dk=lp["dk"], dv=lp["dv"])
        o_f = gla(reverse=False)
        o_b = gla(reverse=True)
        y_mix, xn_mix = _mix_merge_call(
            ya.reshape(b * s, -1), o_f.reshape(b * s, -1), o_b.reshape(b * s, -1), z, lp["norm_w"],
            y, ln, lp["w_pa"], lp["w_pb"], lp["w_out"], lp["ln_mix"],
            col_og=cols["og"], col_ma=cols["ma"], col_mb=cols["mb"], alpha=alpha, dv=lp["dv"])
        hdn = _ffn_up_call(xn_mix, lp["w_up"], lp["conv_ff_w"], lp["conv_ff_b"], s)
        last = li == len(layers) - 1
        outs = _ffn_down_call(hdn, lp["w_down"], y_mix, lp["ln_mix"], lp["ln_ffn"], alpha=alpha,
                              last=last)
        if last:
            return outs[0].reshape(b, s, d)
        y, xn = outs
        ln = lp["ln_ffn"]


def kernel(x_prompt, x_sample, ln_in_g, ln_in_b, w_in, b_in, conv_rg_w, conv_rg_b, rg_wa, rg_ba, rg_wx, rg_bx, rg_lam, gla_wg2, gla_bg, gla_norm_w, w_proj_a, w_proj_b, w_out, ln_mix_g, ln_mix_b, w_up, conv_ff_w, conv_ff_b, w_down, ln_ffn_g, ln_ffn_b):
    p = dict(w_in=w_in, b_in=b_in, conv_rg_w=conv_rg_w, conv_rg_b=conv_rg_b, rg_wa=rg_wa,
             rg_ba=rg_ba, rg_wx=rg_wx, rg_bx=rg_bx, rg_lam=rg_lam, gla_wg2=gla_wg2, gla_bg=gla_bg,
             gla_norm_w=gla_norm_w, w_proj_a=w_proj_a, w_proj_b=w_proj_b, w_out=w_out,
             ln_mix_g=ln_mix_g, ln_mix_b=ln_mix_b, w_up=w_up, conv_ff_w=conv_ff_w,
             conv_ff_b=conv_ff_b, w_down=w_down, ln_ffn_g=ln_ffn_g, ln_ffn_b=ln_ffn_b)
    depth = w_in.shape[0]
    d_model = w_in.shape[1]
    alpha = (2.0 * depth) ** 0.25
    layers = [_prepare_layer(p, l, d_model) for l in range(depth)]
    ln_in = (ln_in_g[None, :], ln_in_b[None, :])
    return (_trunk(x_prompt, ln_in, layers, alpha), _trunk(x_sample, ln_in, layers, alpha))
```

```python
import functools
import math

import jax
import jax.numpy as jnp
from jax import lax
from jax.experimental import pallas as pl
from jax.experimental.pallas import tpu as pltpu

F32 = jnp.float32
BF16 = jnp.bfloat16

V7X_VMEM_BYTES = 64 * 1024 * 1024
LANES = 128
SUBLANES = 8
BF16_ROWS = 16

N_HEADS = 4
CHUNK = 64
RG_C = 8.0
GATE_NORM = 16.0
EPS = 1e-5
RG_GROUP = 128
SCAN_BLOCK = SUBLANES * SUBLANES
GLA_SUB = 256


def _vmem_limit(nbytes):
    return int(min(max(nbytes, 16 * 1024 * 1024), V7X_VMEM_BYTES - 6 * 1024 * 1024))


def _dot(a, b):
    return jnp.dot(a, b, preferred_element_type=F32)


def _layer_norm(x, g, b):
    mu = jnp.mean(x, axis=-1, keepdims=True)
    xc = x - mu
    var = jnp.mean(xc * xc, axis=-1, keepdims=True)
    return xc * lax.rsqrt(var + EPS) * g + b


def _softplus(x):
    return jnp.maximum(x, 0.0) + jnp.log1p(jnp.exp(-jnp.abs(x)))


def _gelu_tanh(x):
    c = math.sqrt(2.0 / math.pi)
    return 0.5 * x * (1.0 + jnp.tanh(c * (x + 0.044715 * (x * x * x))))


def _pick_tile(n, target, multiple):
    t = min(n, target)
    while t > multiple and (n % t or t % multiple):
        t -= multiple
    assert n % t == 0 and t % multiple == 0, (n, target, multiple)
    return t


def _resident(shape):
    return pl.BlockSpec(shape, lambda *_: (0,) * len(shape), pipeline_mode=pl.Buffered(1))


def _ln_cast_kernel(y_ref, g_ref, b_ref, o_ref):
    o_ref[...] = _layer_norm(y_ref[...], g_ref[...], b_ref[...]).astype(o_ref.dtype)


def _ln_cast_call(y, g, b):
    t, d = y.shape
    tm = _pick_tile(t, 512, BF16_ROWS)
    return pl.pallas_call(
        _ln_cast_kernel,
        grid=(t // tm,),
        in_specs=[pl.BlockSpec((tm, d), lambda i: (i, 0)), _resident((1, d)), _resident((1, d))],
        out_specs=pl.BlockSpec((tm, d), lambda i: (i, 0)),
        out_shape=jax.ShapeDtypeStruct((t, d), BF16),
        compiler_params=pltpu.CompilerParams(
            dimension_semantics=("parallel",),
            vmem_limit_bytes=_vmem_limit(6 * tm * d * 4)),
        name="ln_cast",
    )(y, g, b)


def _matmul_in_kernel(x_ref, w_ref, bias_ref, wg_ref, bg_ref, wg2_ref, bg2_ref, z_ref, gz_ref,
                      *, gate_rows):
    n_slices = x_ref.shape[0] // gate_rows
    r0 = pl.multiple_of(jnp.minimum(pl.program_id(1), n_slices - 1) * gate_rows, gate_rows)
    zg = _dot(x_ref[pl.ds(r0, gate_rows), :], wg_ref[...]) + bg_ref[...]
    z_ref[...] = _dot(x_ref[...], w_ref[...]) + bias_ref[...]
    gz_ref[pl.ds(r0, gate_rows), :] = _dot(zg.astype(BF16), wg2_ref[...]) + bg2_ref[...]


def _matmul_in_call(xn, w, bias, wg, bg, wg2, bg2):
    t, d = xn.shape
    n = w.shape[1]
    ng = wg2.shape[1]
    tm = _pick_tile(t, 1024, BF16_ROWS)
    tn = _pick_tile(n, 1536, 2 * LANES)
    gate_rows = _pick_tile(tm, tm, BF16_ROWS)
    while tm % (gate_rows // 2) == 0 and (gate_rows // 2) % BF16_ROWS == 0 \
            and (tm // (gate_rows // 2)) <= n // tn:
        gate_rows //= 2
    est = (2 * tm * d * 2 + 2 * d * tn * 2 + 3 * tm * tn * 4 + d * LANES * 2 + tm * LANES * 4
           + 3 * tm * ng * 4)
    return pl.pallas_call(
        functools.partial(_matmul_in_kernel, gate_rows=gate_rows),
        grid=(t // tm, n // tn),
        in_specs=[pl.BlockSpec((tm, d), lambda i, j: (i, 0)),
                  pl.BlockSpec((d, tn), lambda i, j: (0, j)),
                  pl.BlockSpec((1, tn), lambda i, j: (0, j)),
                  _resident((d, LANES)),
                  _resident((1, LANES)),
                  _resident((LANES, ng)),
                  _resident((1, ng))],
        out_specs=[pl.BlockSpec((tm, tn), lambda i, j: (i, j)),
                   pl.BlockSpec((tm, ng), lambda i, j: (i, 0))],
        out_shape=[jax.ShapeDtypeStruct((t, n), F32),
                   jax.ShapeDtypeStruct((t, ng), F32)],
        compiler_params=pltpu.CompilerParams(
            dimension_semantics=("parallel", "arbitrary"),
            vmem_limit_bytes=_vmem_limit(est)),
        name="matmul_in",
    )(xn, w, bias, wg, bg, wg2, bg2)


def _block_transpose(x):
    rows, c = x.shape
    blocks = rows // SCAN_BLOCK
    return jnp.swapaxes(x.reshape(blocks, SUBLANES, SUBLANES, c), 1, 2).reshape(rows, c)


def _block_scan(a, u, reverse):
    rows, c = a.shape
    blocks = rows // SCAN_BLOCK
    at = a.reshape(blocks, SUBLANES, SUBLANES, c)
    ut = u.reshape(blocks, SUBLANES, SUBLANES, c)
    order = list(range(SUBLANES - 1, -1, -1) if reverse else range(SUBLANES))
    p, h = {}, {}
    prev = None
    for j in order:
        if prev is None:
            p[j], h[j] = at[:, j], ut[:, j]
        else:
            p[j], h[j] = at[:, j] * p[prev], at[:, j] * h[prev] + ut[:, j]
        prev = j
    q, e = p[prev], h[prev]
    row = lax.broadcasted_iota(jnp.int32, q.shape, 1)
    for d in (1, 2, 4):
        shift = SUBLANES - d if reverse else d
        valid = (row < SUBLANES - d) if reverse else (row >= d)
        q_s = pltpu.roll(q, shift, 1)
        e_s = pltpu.roll(e, shift, 1)
        e = jnp.where(valid, q * e_s + e, e)
        q = jnp.where(valid, q * q_s, q)
    shift = SUBLANES - 1 if reverse else 1
    first = (row == SUBLANES - 1) if reverse else (row == 0)
    q_in = jnp.where(first, 1.0, pltpu.roll(q, shift, 1))
    e_in = jnp.where(first, 0.0, pltpu.roll(e, shift, 1))
    a_out = jnp.stack([p[j] * q_in for j in range(SUBLANES)], axis=1)
    u_out = jnp.stack([h[j] + p[j] * e_in for j in range(SUBLANES)], axis=1)
    return a_out.reshape(rows, c), u_out.reshape(rows, c)


def _rglru_kernel(zrx_ref, zrg_ref, cw_ref, cb_ref, wg_ref, bg_ref, lam_ref, o_ref,
                  xpad_ref, af_ref, uf_ref, ab_ref, ub_ref, *, tile):
    s, c = zrx_ref.shape
    n_tiles = s // tile
    n_groups = s // SCAN_BLOCK
    pad = SUBLANES

    xpad_ref[0:pad, :] = jnp.zeros((pad, c), F32)
    xpad_ref[pad + s:pad + s + pad, :] = jnp.zeros((pad, c), F32)

    def copy_body(i, _):
        t0 = pl.multiple_of(i * tile, tile)
        xpad_ref[pl.ds(pad + t0, tile), :] = zrx_ref[pl.ds(t0, tile), :]
        return 0

    lax.fori_loop(0, n_tiles, copy_body, 0)

    cw = cw_ref[...]
    cb = cb_ref[...]
    bg = bg_ref[...]
    decay = -RG_C * _softplus(-lam_ref[...])

    def local_body(i, _):
        t0 = pl.multiple_of(i * tile, tile)
        xa = cb
        for j in range(4):
            xa = xa + cw[j:j + 1, :] * xpad_ref[pl.ds(pad + t0 - 2 + j, tile), :]
        xa = _block_transpose(xa)
        gz = _dot(xa.astype(BF16), wg_ref[...]) + bg
        for direction, (a_ref, u_ref) in enumerate(((af_ref, uf_ref), (ab_ref, ub_ref))):
            lo = 2 * c * direction
            r = jax.nn.sigmoid(gz[:, lo:lo + c])
            ig = jax.nn.sigmoid(gz[:, lo + c:lo + 2 * c])
            log_a = r * decay[:, c * direction:c * (direction + 1)]
            a = jnp.exp(log_a)
            th = jnp.tanh(log_a)
            one_minus_a2 = (-2.0 * th) / (1.0 - th)
            root = jnp.where(one_minus_a2 > 0.0, one_minus_a2 * lax.rsqrt(one_minus_a2), 0.0)
            u = root * (ig * xa)
            a_run, u_run = _block_scan(a, u, reverse=bool(direction))
            a_ref[pl.ds(t0, tile), :] = a_run
            u_ref[pl.ds(t0, tile), :] = u_run
        return 0

    lax.fori_loop(0, n_tiles, local_body, 0)

    def chain_body(g, carry):
        h_f, h_b = carry
        rf = pl.multiple_of(g * SCAN_BLOCK, SCAN_BLOCK)
        rb = pl.multiple_of((n_groups - 1 - g) * SCAN_BLOCK, SCAN_BLOCK)
        hf = af_ref[pl.ds(rf, SCAN_BLOCK), :] * h_f + uf_ref[pl.ds(rf, SCAN_BLOCK), :]
        uf_ref[pl.ds(rf, SCAN_BLOCK), :] = hf
        hb = ab_ref[pl.ds(rb, SCAN_BLOCK), :] * h_b + ub_ref[pl.ds(rb, SCAN_BLOCK), :]
        ub_ref[pl.ds(rb, SCAN_BLOCK), :] = hb
        return hf[SCAN_BLOCK - 1:SCAN_BLOCK], hb[0:1]

    zero = jnp.zeros((1, c), F32)
    lax.fori_loop(0, n_groups, chain_body, (zero, zero), unroll=4)

    def out_body(i, _):
        t0 = pl.multiple_of(i * tile, tile)
        h = _block_transpose(uf_ref[pl.ds(t0, tile), :] + ub_ref[pl.ds(t0, tile), :])
        o_ref[pl.ds(t0, tile), :] = (_gelu_tanh(zrg_ref[pl.ds(t0, tile), :]) * h).astype(o_ref.dtype)
        return 0

    lax.fori_loop(0, n_tiles, out_body, 0)


def _rglru_call(z3, cw, cb, wg, bg, lam, d_rnn):
    b, s, _ = z3.shape
    c = RG_GROUP
    n_groups = d_rnn // c
    tile = _pick_tile(s, 512, SCAN_BLOCK)
    est = (2 * 2 * s * c * 4 + 2 * s * c * 2 + 5 * s * c * 4 + 2 * SUBLANES * c * 4
           + 40 * tile * c * 4 + 4 * c * 4 * c * 2)
    return pl.pallas_call(
        functools.partial(_rglru_kernel, tile=tile),
        grid=(b, n_groups),
        in_specs=[pl.BlockSpec((None, s, c), lambda i, j: (i, 0, j)),
                  pl.BlockSpec((None, s, c), lambda i, j: (i, 0, n_groups + j)),
                  pl.BlockSpec((None, 4, c), lambda i, j: (j, 0, 0)),
                  pl.BlockSpec((None, 1, c), lambda i, j: (j, 0, 0)),
                  pl.BlockSpec((None, c, 4 * c), lambda i, j: (j, 0, 0)),
                  pl.BlockSpec((None, 1, 4 * c), lambda i, j: (j, 0, 0)),
                  pl.BlockSpec((None, 1, 2 * c), lambda i, j: (j, 0, 0))],
        out_specs=pl.BlockSpec((None, s, c), lambda i, j: (i, 0, j)),
        out_shape=jax.ShapeDtypeStruct((b, s, d_rnn), BF16),
        scratch_shapes=[pltpu.VMEM((s + 2 * SUBLANES, c), F32)] + [pltpu.VMEM((s, c), F32)] * 4,
        compiler_params=pltpu.CompilerParams(
            dimension_semantics=("parallel", "parallel"),
            vmem_limit_bytes=_vmem_limit(est)),
        name="rglru",
    )(z3, z3, cw, cb, wg, bg, lam)


def _chunk_cumsum(g, reverse):
    rows, c = g.shape
    groups = rows // SUBLANES
    per_chunk = CHUNK // SUBLANES
    x = g.reshape(groups, SUBLANES, c)
    row = lax.broadcasted_iota(jnp.int32, (groups, SUBLANES, c), 1)
    for d in (1, 2, 4):
        shift = SUBLANES - d if reverse else d
        valid = (row < SUBLANES - d) if reverse else (row >= d)
        x = x + jnp.where(valid, pltpu.roll(x, shift, 1), 0.0)
    outs = [None] * groups
    for ch in range(groups // per_chunk):
        carry = None
        order = range(per_chunk - 1, -1, -1) if reverse else range(per_chunk)
        for gi in order:
            idx = ch * per_chunk + gi
            xg = x[idx] if carry is None else x[idx] + carry
            outs[idx] = xg
            carry = xg[0:1] if reverse else xg[SUBLANES - 1:SUBLANES]
    return jnp.concatenate(outs, axis=0)


def _gla_direction(q_ref, k_ref, v_ref, gz_ref, o_ref, base_ref, st, *, reverse, scale):
    rows, dk = q_ref.shape
    pair = 2 * CHUNK
    assert dk == LANES and rows % pair == 0

    ti = lax.broadcasted_iota(jnp.int32, (CHUNK, pair), 0)
    si = lax.broadcasted_iota(jnp.int32, (CHUNK, pair), 1)
    h_a = 1 if reverse else 0
    h_b = 1 - h_a
    own = {}
    for half in range(2):
        local = si - half * CHUNK
        causal = (local >= ti) if reverse else (local <= ti)
        own[half] = (local >= 0) & (local < CHUNK) & causal
    mask_a = own[h_a]
    other_half = (si >= h_a * CHUNK) & (si < (h_a + 1) * CHUNK)
    mask_b = own[h_b] | other_half

    def by_half(x_a, x_b, axis=0):
        return jnp.concatenate([x_a, x_b] if h_a == 0 else [x_b, x_a], axis=axis)

    nt = (((1,), (1,)), ((), ()))
    tn = (((0,), (0,)), ((), ()))

    sub = min(GLA_SUB, rows)
    n_sub = rows // sub
    per_sub = sub // CHUNK
    n_pairs = per_sub // 2
    sub_order = list(range(n_sub - 1, -1, -1) if reverse else range(n_sub))
    pair_order = list(range(n_pairs - 1, -1, -1) if reverse else range(n_pairs))

    def stage_a(sb):
        r0 = sb * sub
        gz = gz_ref[r0:r0 + sub, :]
        g = (jnp.minimum(gz, 0.0) - jnp.log(1.0 + jnp.exp(-jnp.abs(gz)))) * (1.0 / GATE_NORM)
        gc_s = _chunk_cumsum(g, reverse)
        g_last = [gc_s[c * CHUNK:c * CHUNK + 1] if reverse
                  else gc_s[(c + 1) * CHUNK - 1:(c + 1) * CHUNK] for c in range(per_sub)]
        g_last_rows = jnp.concatenate([jnp.broadcast_to(r, (CHUNK, dk)) for r in g_last], axis=0)
        g_pair = [g_last[2 * p] + g_last[2 * p + 1] for p in range(n_pairs)]
        dec_t = jnp.exp(jnp.concatenate(g_pair + [jnp.zeros((LANES - n_pairs, dk), F32)], axis=0).T)
        k = k_ref[r0:r0 + sub, :]
        qd = q_ref[r0:r0 + sub, :] * scale * jnp.exp(gc_s)
        kd = (k * jnp.exp(-gc_s)).astype(BF16)
        k_end = k * jnp.exp(g_last_rows - gc_s)
        out = []
        for p in pair_order:
            ra = slice((2 * p + h_a) * CHUNK, (2 * p + h_a + 1) * CHUNK)
            rb = slice((2 * p + h_b) * CHUNK, (2 * p + h_b + 1) * CHUNK)
            dec_a = jnp.exp(g_last[2 * p + h_a])
            dec_b = jnp.exp(g_last[2 * p + h_b])
            q_a = qd[ra].astype(BF16)
            q_b = qd[rb].astype(BF16)
            v_pair = v_ref[r0 + p * pair:r0 + (p + 1) * pair, :].astype(BF16)
            att_a = lax.dot_general(q_a, kd[p * pair:(p + 1) * pair], nt, preferred_element_type=F32)
            att_a = jnp.where(mask_a, att_a, 0.0).astype(BF16)
            keys_b = by_half(k_end[ra].astype(BF16), kd[rb])
            att_b = lax.dot_general(q_b, keys_b, nt, preferred_element_type=F32)
            att_b = jnp.where(mask_b, att_b, 0.0).astype(BF16)
            keys_s = by_half((k_end[ra] * dec_b).astype(BF16), k_end[rb].astype(BF16))
            kv = lax.dot_general(keys_s, v_pair, tn, preferred_element_type=F32)
            lhs = by_half(jnp.concatenate([q_a, att_a], axis=1),
                          jnp.concatenate([(qd[rb] * dec_a).astype(BF16), att_b], axis=1))
            out.append(dict(row0=r0 + p * pair, lhs=lhs, v_pair=v_pair, kv=kv,
                            dec=dec_t[:, p:p + 1]))
        return out

    def stage_bc(pairs, st):
        st_in = []
        for a in pairs:
            st_in.append(st.astype(BF16))
            st = a["dec"] * st + a["kv"]
        for a, s_in in zip(pairs, st_in):
            out_rows = slice(a["row0"], a["row0"] + pair)
            o = _dot(a["lhs"], jnp.concatenate([s_in, a["v_pair"]], axis=0))
            o_ref[out_rows, :] = o if base_ref is None else base_ref[out_rows, :] + o
        return st

    ahead = stage_a(sub_order[0])
    for i in range(n_sub):
        cur = ahead
        if i + 1 < n_sub:
            ahead = stage_a(sub_order[i + 1])
        st = stage_bc(cur, st)
    return st


def _gla_both_kernel(q_ref, k_ref, v_ref, gzf_ref, gzb_ref, o_ref, *, scale):
    zero = jnp.zeros((q_ref.shape[1], v_ref.shape[1]), F32)
    _gla_direction(q_ref, k_ref, v_ref, gzf_ref, o_ref, None, zero, reverse=False, scale=scale)
    _gla_direction(q_ref, k_ref, v_ref, gzb_ref, o_ref, o_ref, zero, reverse=True, scale=scale)


def _gla_one_kernel(q_ref, k_ref, v_ref, gz_ref, *refs, reverse, scale):
    base_ref = refs[0] if len(refs) == 3 else None
    o_ref, st_ref = refs[-2:]

    @pl.when(pl.program_id(2) == 0)
    def _():
        st_ref[...] = jnp.zeros_like(st_ref)

    st_ref[...] = _gla_direction(q_ref, k_ref, v_ref, gz_ref, o_ref, base_ref, st_ref[...],
                                 reverse=reverse, scale=scale)


def _gla_call(z3, gz3, *, col_q, col_k, col_v, dk, dv):
    b, s, _ = z3.shape
    rows = _pick_tile(s, 2048, 2 * CHUNK)
    nb = s // rows
    est = 2 * rows * (4 * dk + 3 * dv) * 4 + 40 * rows * dk * 4 + 24 * dv * dk * 4
    out_shape = jax.ShapeDtypeStruct((b, s, N_HEADS * dv), F32)

    def specs(reverse):
        blk = (lambda j: nb - 1 - j) if reverse else (lambda j: j)
        col_g = N_HEADS * dk if reverse else 0
        qkv = [pl.BlockSpec((None, rows, dk), lambda i, h, j: (i, blk(j), col_q // dk + h)),
               pl.BlockSpec((None, rows, dk), lambda i, h, j: (i, blk(j), col_k // dk + h)),
               pl.BlockSpec((None, rows, dv), lambda i, h, j: (i, blk(j), col_v // dv + h))]
        gate = pl.BlockSpec((None, rows, dk), lambda i, h, j: (i, blk(j), col_g // dk + h))
        out = pl.BlockSpec((None, rows, dv), lambda i, h, j: (i, blk(j), h))
        return qkv, gate, out

    if nb == 1:
        qkv, gate_f, out = specs(False)
        _, gate_b, _ = specs(True)
        return pl.pallas_call(
            functools.partial(_gla_both_kernel, scale=dk ** -0.5),
            grid=(b, N_HEADS, nb),
            in_specs=qkv + [gate_f, gate_b],
            out_specs=out,
            out_shape=out_shape,
            compiler_params=pltpu.CompilerParams(
                dimension_semantics=("parallel", "parallel", "arbitrary"),
                vmem_limit_bytes=_vmem_limit(est)),
            name="gla_both",
        )(z3, z3, z3, gz3, gz3)

    def one(reverse, base):
        qkv, gate, out = specs(reverse)
        return pl.pallas_call(
            functools.partial(_gla_one_kernel, reverse=reverse, scale=dk ** -0.5),
            grid=(b, N_HEADS, nb),
            in_specs=qkv + [gate] + ([] if base is None else [out]),
            out_specs=out,
            out_shape=out_shape,
            scratch_shapes=[pltpu.VMEM((dk, dv), F32)],
            compiler_params=pltpu.CompilerParams(
                dimension_semantics=("parallel", "parallel", "arbitrary"),
                vmem_limit_bytes=_vmem_limit(est)),
            name="gla_bwd" if reverse else "gla_fwd",
        )(z3, z3, z3, gz3, *([] if base is None else [base]))

    return one(True, one(False, None))


def _mix_merge_kernel(ya_ref, og_in_ref, og_ref, nw_ref, ma_ref, mb_ref, y_ref, g_ref, b_ref,
                      wa_ref, wb_ref, wo_ref, g2_ref, b2_ref, o_ref, xn_ref, *, alpha, dv):
    y_a = _dot(ya_ref[...], wa_ref[...])
    o = og_in_ref[...]
    parts = []
    for h in range(o.shape[1] // dv):
        oh = o[:, h * dv:(h + 1) * dv]
        parts.append(oh * lax.rsqrt(jnp.mean(oh * oh, axis=-1, keepdims=True) + EPS))
    og = og_ref[...]
    yb = (jnp.concatenate(parts, axis=1) * nw_ref[...] * (og * jax.nn.sigmoid(og))).astype(BF16)
    y_b = _dot(yb, wb_ref[...])
    merged = jax.nn.sigmoid(ma_ref[...]) * y_a + jax.nn.sigmoid(mb_ref[...]) * y_b
    mix = _dot(merged.astype(BF16), wo_ref[...])
    y = alpha * _layer_norm(y_ref[...], g_ref[...], b_ref[...]) + mix
    o_ref[...] = y
    xn_ref[...] = _layer_norm(y, g2_ref[...], b2_ref[...]).astype(xn_ref.dtype)


def _mix_merge_call(ya, o_gla, z, nw, y, ln, wa, wb, wo, ln_next, *, col_og, col_ma, col_mb,
                    alpha, dv):
    t, d = y.shape
    dr = ya.shape[1]
    tm = _pick_tile(t, 256, BF16_ROWS)
    est = (2 * tm * dr * 2 + 2 * 2 * tm * dr * 4 + 2 * 4 * tm * d * 4 + 2 * tm * d * 2
           + (2 * dr * d + d * d) * 2 + 8 * tm * d * 4)
    row = lambda i: (i, 0)
    return pl.pallas_call(
        functools.partial(_mix_merge_kernel, alpha=alpha, dv=dv),
        grid=(t // tm,),
        in_specs=[pl.BlockSpec((tm, dr), row),
                  pl.BlockSpec((tm, dr), row),
                  pl.BlockSpec((tm, dr), lambda i: (i, col_og // dr)),
                  _resident((1, dr)),
                  pl.BlockSpec((tm, d), lambda i: (i, col_ma // d)),
                  pl.BlockSpec((tm, d), lambda i: (i, col_mb // d)),
                  pl.BlockSpec((tm, d), row),
                  _resident((1, d)), _resident((1, d)),
                  _resident((dr, d)), _resident((dr, d)), _resident((d, d)),
                  _resident((1, d)), _resident((1, d))],
        out_specs=[pl.BlockSpec((tm, d), row), pl.BlockSpec((tm, d), row)],
        out_shape=[jax.ShapeDtypeStruct((t, d), F32), jax.ShapeDtypeStruct((t, d), BF16)],
        compiler_params=pltpu.CompilerParams(
            dimension_semantics=("parallel",),
            vmem_limit_bytes=_vmem_limit(est)),
        name="mix_merge",
    )(ya, o_gla, z, nw, z, z, y, ln[0], ln[1], wa, wb, wo, ln_next[0], ln_next[1])


def _ffn_up_kernel(x_ref, xp_ref, xq_ref, wg_ref, wv_ref, cw_ref, cb_ref, o_ref, xs_ref,
                   *, tiles_per_seq):
    tm = x_ref.shape[0]
    halo = xp_ref.shape[0]
    i = pl.program_id(0)

    @pl.when(pl.program_id(1) == 0)
    def _():
        xs_ref[halo:halo + tm, :] = x_ref[...]
        has_prev = (i % tiles_per_seq) != 0
        has_next = (i % tiles_per_seq) != tiles_per_seq - 1
        xs_ref[0:halo, :] = jnp.where(has_prev, xp_ref[...], jnp.zeros_like(xp_ref))
        xs_ref[halo + tm:halo + tm + halo, :] = jnp.where(has_next, xq_ref[...], jnp.zeros_like(xq_ref))

    ug = _dot(xs_ref[...], wg_ref[...])
    uv = _dot(x_ref[...], wv_ref[...])
    n_rows = ug.shape[0]
    cw = cw_ref[...]
    conv = (cb_ref[...] + cw[0:1, :] * pltpu.roll(ug, 1, 0) + cw[1:2, :] * ug
            + cw[2:3, :] * pltpu.roll(ug, n_rows - 1, 0))
    conv = conv[halo:halo + tm, :]
    o_ref[...] = (_gelu_tanh(conv) * uv).astype(o_ref.dtype)


def _ffn_up_call(xn, w_up, cw, cb, seq_len):
    t, d = xn.shape
    d_ff = w_up.shape[1] // 2
    halo = BF16_ROWS
    tm = _pick_tile(seq_len, 1024, halo)
    tn = _pick_tile(d_ff, 768, LANES)
    tiles_per_seq = seq_len // tm
    hb = tm // halo
    n_hblocks = t // halo
    n_j = d_ff // tn
    est = (2 * tm * d * 2 + 4 * halo * d * 2 + (tm + 2 * halo) * d * 2 + 2 * 2 * d * tn * 2
           + 2 * tm * tn * 2 + 7 * (tm + 2 * halo) * tn * 4)
    return pl.pallas_call(
        functools.partial(_ffn_up_kernel, tiles_per_seq=tiles_per_seq),
        grid=(t // tm, n_j),
        in_specs=[pl.BlockSpec((tm, d), lambda i, j: (i, 0)),
                  pl.BlockSpec((halo, d), lambda i, j: (jnp.maximum(i * hb - 1, 0), 0)),
                  pl.BlockSpec((halo, d), lambda i, j: (jnp.minimum((i + 1) * hb, n_hblocks - 1), 0)),
                  pl.BlockSpec((d, tn), lambda i, j: (0, j)),
                  pl.BlockSpec((d, tn), lambda i, j: (0, n_j + j)),
                  pl.BlockSpec((3, tn), lambda i, j: (0, j)),
                  pl.BlockSpec((1, tn), lambda i, j: (0, j))],
        out_specs=pl.BlockSpec((tm, tn), lambda i, j: (i, j)),
        out_shape=jax.ShapeDtypeStruct((t, d_ff), BF16),
        scratch_shapes=[pltpu.VMEM((tm + 2 * halo, d), BF16)],
        compiler_params=pltpu.CompilerParams(
            dimension_semantics=("parallel", "arbitrary"),
            vmem_limit_bytes=_vmem_limit(est)),
        name="ffn_up",
    )(xn, xn, xn, w_up, w_up, cw, cb)


def _ffn_down_kernel(h_ref, w_ref, y_ref, g_ref, b_ref, g2_ref, b2_ref, *out_refs, alpha, last):
    y = alpha * _layer_norm(y_ref[...], g_ref[...], b_ref[...]) + _dot(h_ref[...], w_ref[...])
    xn = _layer_norm(y, g2_ref[...], b2_ref[...])
    if last:
        out_refs[0][...] = xn
    else:
        out_refs[0][...] = y
        out_refs[1][...] = xn.astype(out_refs[1].dtype)


def _ffn_down_call(h, w, y, ln, ln_next, *, alpha, last):
    t, d = y.shape
    d_ff = h.shape[1]
    tm = _pick_tile(t, 256, BF16_ROWS)
    est = d_ff * d * 2 + 2 * tm * d_ff * 2 + 4 * tm * d * 4 + 2 * tm * d * 2 + 6 * tm * d * 4
    row = lambda i: (i, 0)
    out_specs = [pl.BlockSpec((tm, d), row)]
    out_shape = [jax.ShapeDtypeStruct((t, d), F32)]
    if not last:
        out_specs.append(pl.BlockSpec((tm, d), row))
        out_shape.append(jax.ShapeDtypeStruct((t, d), BF16))
    return pl.pallas_call(
        functools.partial(_ffn_down_kernel, alpha=alpha, last=last),
        grid=(t // tm,),
        in_specs=[pl.BlockSpec((tm, d_ff), row),
                  _resident((d_ff, d)),
                  pl.BlockSpec((tm, d), row),
                  _resident((1, d)), _resident((1, d)), _resident((1, d)), _resident((1, d))],
        out_specs=out_specs,
        out_shape=out_shape,
        compiler_params=pltpu.CompilerParams(
            dimension_semantics=("parallel",),
            vmem_limit_bytes=_vmem_limit(est)),
        name="ffn_down",
    )(h, w, y, ln[0], ln[1], ln_next[0], ln_next[1])


def _prepare_layer(p, l, d_model):
    d_rnn = p["conv_rg_w"].shape[-1]
    dk_tot = p["gla_wg2"].shape[-1]
    rank = p["gla_wg2"].shape[-2]
    dv_tot = p["w_proj_b"].shape[1]
    block_w = p["rg_wa"].shape[3]
    dk, dv = dk_tot // N_HEADS, dv_tot // N_HEADS

    sizes = [d_rnn, d_rnn, dk_tot, dk_tot, dv_tot, dv_tot, rank, rank, d_model, d_model]
    offs = [0]
    for sz in sizes:
        offs.append(offs[-1] + sz)
    rx, rg, q, k, v, og, gf, gb, ma, mb = [slice(offs[i], offs[i + 1]) for i in range(10)]
    order = [rx, rg, ma, mb, q, k, v, og]
    w_in, b_in = p["w_in"][l], p["b_in"][l]
    w_main = jnp.concatenate([w_in[:, s] for s in order], axis=1).astype(BF16)
    b_main = jnp.concatenate([b_in[s] for s in order])[None, :]
    cols, c0 = {}, 0
    for name, s in zip(("rx", "rg", "ma", "mb", "q", "k", "v", "og"), order):
        cols[name] = c0
        c0 += s.stop - s.start
    w_gate = jnp.zeros((d_model, LANES), F32).at[:, :2 * rank].set(w_in[:, gf.start:gb.stop]).astype(BF16)
    b_gate = jnp.zeros((1, LANES), F32).at[0, :2 * rank].set(b_in[gf.start:gb.stop])

    c = RG_GROUP
    n_groups = d_rnn // c
    per = c // block_w

    def group_dense(w):
        wgrp = w.reshape(n_groups, per, block_w, block_w)
        eye = jnp.eye(per, dtype=w.dtype)
        return jnp.einsum("gpij,pq->gpiqj", wgrp, eye).reshape(n_groups, c, c)

    rg_w = jnp.concatenate([group_dense(p["rg_wa"][l, 0]), group_dense(p["rg_wx"][l, 0]),
                            group_dense(p["rg_wa"][l, 1]), group_dense(p["rg_wx"][l, 1])],
                           axis=2).astype(BF16)

    def grp(vec):
        return vec.reshape(n_groups, 1, c)

    rg_b = jnp.concatenate([grp(p["rg_ba"][l, 0]), grp(p["rg_bx"][l, 0]),
                            grp(p["rg_ba"][l, 1]), grp(p["rg_bx"][l, 1])], axis=2)
    rg_lam = jnp.concatenate([grp(p["rg_lam"][l, 0]), grp(p["rg_lam"][l, 1])], axis=2)
    conv_w = p["conv_rg_w"][l].reshape(-1, n_groups, c).transpose(1, 0, 2)
    conv_b = grp(p["conv_rg_b"][l])

    w_gate2 = jnp.zeros((LANES, 2 * dk_tot), F32)
    w_gate2 = w_gate2.at[:rank, :dk_tot].set(p["gla_wg2"][l, 0])
    w_gate2 = w_gate2.at[rank:2 * rank, dk_tot:].set(p["gla_wg2"][l, 1]).astype(BF16)
    b_gate2 = jnp.concatenate([p["gla_bg"][l, 0], p["gla_bg"][l, 1]])[None, :]

    return dict(
        cols=cols, d_rnn=d_rnn, dk=dk, dv=dv,
        w_main=w_main, b_main=b_main, w_gate=w_gate, b_gate=b_gate,
        rg_w=rg_w, rg_b=rg_b, rg_lam=rg_lam, conv_w=conv_w, conv_b=conv_b,
        w_gate2=w_gate2, b_gate2=b_gate2,
        norm_w=jnp.tile(p["gla_norm_w"][l], N_HEADS)[None, :],
        w_pa=p["w_proj_a"][l].astype(BF16), w_pb=p["w_proj_b"][l].astype(BF16),
        w_out=p["w_out"][l].astype(BF16),
        ln_mix=(p["ln_mix_g"][l][None, :], p["ln_mix_b"][l][None, :]),
        w_up=p["w_up"][l].astype(BF16), conv_ff_w=p["conv_ff_w"][l],
        conv_ff_b=p["conv_ff_b"][l][None, :], w_down=p["w_down"][l].astype(BF16),
        ln_ffn=(p["ln_ffn_g"][l][None, :], p["ln_ffn_b"][l][None, :]),
    )


def _trunk(x, ln_in, layers, alpha):
    b, s, d = x.shape
    y = x.reshape(b * s, d)
    ln = ln_in
    xn = _ln_cast_call(y, ln[0], ln[1])
    for li, lp in enumerate(layers):
        cols = lp["cols"]
        z, gz = _matmul_in_call(xn, lp["w_main"], lp["b_main"], lp["w_gate"], lp["b_gate"],
                                lp["w_gate2"], lp["b_gate2"])
        z3 = z.reshape(b, s, z.shape[1])
        gz3 = gz.reshape(b, s, gz.shape[1])
        ya = _rglru_call(z3, lp["conv_w"], lp["conv_b"], lp["rg_w"], lp["rg_b"], lp["rg_lam"],
                         lp["d_rnn"])
        o_gla = _gla_call(z3, gz3, col_q=cols["q"], col_k=cols["k"], col_v=cols["v"],
                          dk=lp["dk"], dv=lp["dv"])
        y_mix, xn_mix = _mix_merge_call(
            ya.reshape(b * s, -1), o_gla.reshape(b * s, -1), z, lp["norm_w"],
            y, ln, lp["w_pa"], lp["w_pb"], lp["w_out"], lp["ln_mix"],
            col_og=cols["og"], col_ma=cols["ma"], col_mb=cols["mb"], alpha=alpha, dv=lp["dv"])
        hdn = _ffn_up_call(xn_mix, lp["w_up"], lp["conv_ff_w"], lp["conv_ff_b"], s)
        last = li == len(layers) - 1
        outs = _ffn_down_call(hdn, lp["w_down"], y_mix, lp["ln_mix"], lp["ln_ffn"], alpha=alpha,
                              last=last)
        if last:
            return outs[0].reshape(b, s, d)
        y, xn = outs
        ln = lp["ln_ffn"]


def kernel(x_prompt, x_sample, ln_in_g, ln_in_b, w_in, b_in, conv_rg_w, conv_rg_b, rg_wa, rg_ba, rg_wx, rg_bx, rg_lam, gla_wg2, gla_bg, gla_norm_w, w_proj_a, w_proj_b, w_out, ln_mix_g, ln_mix_b, w_up, conv_ff_w, conv_ff_b, w_down, ln_ffn_g, ln_ffn_b):
    p = dict(w_in=w_in, b_in=b_in, conv_rg_w=conv_rg_w, conv_rg_b=conv_rg_b, rg_wa=rg_wa,
             rg_ba=rg_ba, rg_wx=rg_wx, rg_bx=rg_bx, rg_lam=rg_lam, gla_wg2=gla_wg2, gla_bg=gla_bg,
             gla_norm_w=gla_norm_w, w_proj_a=w_proj_a, w_proj_b=w_proj_b, w_out=w_out,
             ln_mix_g=ln_mix_g, ln_mix_b=ln_mix_b, w_up=w_up, conv_ff_w=conv_ff_w,
             conv_ff_b=conv_ff_b, w_down=w_down, ln_ffn_g=ln_ffn_g, ln_ffn_b=ln_ffn_b)
    depth = w_in.shape[0]
    d_model = w_in.shape[1]
    alpha = (2.0 * depth) ** 0.25
    layers = [_prepare_layer(p, l, d_model) for l in range(depth)]
    ln_in = (ln_in_g[None, :], ln_in_b[None, :])
    return (_trunk(x_prompt, ln_in, layers, alpha), _trunk(x_sample, ln_in, layers, alpha))
```
